```python
import math
import jax, jax.numpy as jnp
from jax import lax
import numpy as np

D_MODEL = 1024
BATCH = 8
SEQ = 4096
DEPTH = 4

FOX_HEADS = 8
FOX_HEAD_DIM = 64
FOX_WIDTH = FOX_HEADS * FOX_HEAD_DIM
Q_BLOCK = 128
RET_HEADS = 8
RET_QK_DIM = 64
RET_V_DIM = 128
RET_QK_WIDTH = RET_HEADS * RET_QK_DIM
RET_V_WIDTH = RET_HEADS * RET_V_DIM
RET_CHUNK = 128
ROPE_BASE = 10000.0
N_BRANCH = 2
IN_COLS = 3 * FOX_WIDTH + FOX_HEADS + 2 * RET_QK_WIDTH + 2 * RET_V_WIDTH + N_BRANCH * D_MODEL
D_FF = 2816
N_EXPERTS = 8
TOP_K = 2
D_FF_EXPERT = 3584
N_DENSE = (DEPTH + 1) // 2
N_MOE = DEPTH // 2
DN_ALPHA = (2 * DEPTH) ** 0.25
DN_BETA = (8 * DEPTH) ** -0.25
LN_EPS = 1e-5
RMS_EPS = 1e-6

kernel_name = "fox_retnet_gated_hybrid_deepnorm_moe"

f32 = jnp.float32


def layer_norm(x, g, b):
    xf = x.astype(f32)
    mu = jnp.mean(xf, axis=-1, keepdims=True)
    var = jnp.mean(jnp.square(xf - mu), axis=-1, keepdims=True)
    y = (xf - mu) * lax.rsqrt(var + LN_EPS) * g.astype(f32) + b.astype(f32)
    return y.astype(x.dtype)


def rotary(x, positions):
    half = x.shape[-1] // 2
    inv_freq = ROPE_BASE ** (-jnp.arange(half, dtype=f32) / half)
    ang = positions.astype(f32)[..., None] * inv_freq
    cos = jnp.cos(ang)[:, :, None, :]
    sin = jnp.sin(ang)[:, :, None, :]
    x1 = x[..., :half].astype(f32)
    x2 = x[..., half:].astype(f32)
    return jnp.concatenate([x1 * cos - x2 * sin, x1 * sin + x2 * cos], axis=-1).astype(x.dtype)


def forgetting_attention(q, k, v, log_f):
    B, S, H, dh = q.shape
    n_blk = S // Q_BLOCK
    c = jnp.cumsum(log_f, axis=1).transpose(0, 2, 1)
    kh = k.transpose(0, 2, 1, 3)
    vh = v.transpose(0, 2, 1, 3)
    qb = q.reshape(B, n_blk, Q_BLOCK, H, dh).transpose(1, 0, 3, 2, 4)
    cb = c.reshape(B, H, n_blk, Q_BLOCK).transpose(2, 0, 1, 3)
    key_pos = jnp.arange(S)
    scale = dh ** -0.5

    def one_block(args):
        i, q_i, c_i = args
        s = jnp.einsum('bhqd,bhkd->bhqk', q_i, kh, preferred_element_type=f32) * scale
        s = s + c_i[..., :, None] - c[..., None, :]
        q_pos = i * Q_BLOCK + jnp.arange(Q_BLOCK)
        causal = key_pos[None, :] <= q_pos[:, None]
        s = jnp.where(causal, s, -jnp.inf)
        p = jax.nn.softmax(s, axis=-1)
        return jnp.einsum('bhqk,bhkd->bhqd', p.astype(vh.dtype), vh)

    out = lax.map(one_block, (jnp.arange(n_blk), qb, cb))
    return out.transpose(1, 0, 3, 2, 4).reshape(B, S, H * dh)


def retention_decays():
    return jnp.log(1.0 - 2.0 ** (-5.0 - jnp.arange(RET_HEADS, dtype=f32)))


def chunkwise_retention(q, k, v, log_gamma):
    B, S, H, dk = q.shape
    dv = v.shape[-1]
    C = RET_CHUNK
    N = S // C
    qc = q.astype(f32).reshape(B, N, C, H, dk)
    kc = k.astype(f32).reshape(B, N, C, H, dk)
    vc = v.astype(f32).reshape(B, N, C, H, dv)
    idx = jnp.arange(C, dtype=f32)
    diff = idx[:, None] - idx[None, :]
    d_intra = jnp.where(diff >= 0,
                        jnp.exp(jnp.maximum(diff, 0.0)[None] * log_gamma[:, None, None]),
                        0.0)
    scores = jnp.einsum('bnihd,bnjhd->bnhij', qc, kc) * d_intra
    intra = jnp.einsum('bnhij,bnjhe->bnihe', scores, vc)
    k_dec = jnp.exp((C - 1 - idx)[:, None] * log_gamma[None, :])
    kv = jnp.einsum('bnjhd,bnjhe->bnhde', kc * k_dec[:, :, None], vc)
    chunk_decay = jnp.exp(C * log_gamma)[:, None, None]

    def step(state, kv_n):
        return chunk_decay * state + kv_n, state

    _, prev = lax.scan(step, jnp.zeros((B, H, dk, dv), f32), jnp.moveaxis(kv, 1, 0))
    prev = jnp.moveaxis(prev, 0, 1)
    q_dec = jnp.exp((idx + 1.0)[:, None] * log_gamma[None, :])
    cross = jnp.einsum('bnihd,bnhde->bnihe', qc * q_dec[:, :, None], prev)
    return (intra + cross).reshape(B, S, H, dv)


def head_rms_norm(y, g, out_dtype):
    B, S, H, dv = y.shape
    yf = y.astype(f32)
    yf = yf * lax.rsqrt(jnp.mean(yf * yf, axis=-1, keepdims=True) + RMS_EPS)
    return (yf.reshape(B, S, H * dv) * g.astype(f32)).astype(out_dtype)


def hybrid_mixer(x, positions, w_in, b_forget, ret_norm_g, w_branch_fox, w_branch_ret, w_out):
    B, S, _ = x.shape
    proj = x @ w_in
    o = 0
    q_a = proj[..., o:o + FOX_WIDTH]; o += FOX_WIDTH
    k_a = proj[..., o:o + FOX_WIDTH]; o += FOX_WIDTH
    v_a = proj[..., o:o + FOX_WIDTH]; o += FOX_WIDTH
    f_logit = proj[..., o:o + FOX_HEADS]; o += FOX_HEADS
    q_b = proj[..., o:o + RET_QK_WIDTH]; o += RET_QK_WIDTH
    k_b = proj[..., o:o + RET_QK_WIDTH]; o += RET_QK_WIDTH
    v_b = proj[..., o:o + RET_V_WIDTH]; o += RET_V_WIDTH
    g_b = proj[..., o:o + RET_V_WIDTH]; o += RET_V_WIDTH
    gate_logits = proj[..., o:o + N_BRANCH * D_MODEL]

    log_f = jax.nn.log_sigmoid(f_logit.astype(f32) + b_forget.astype(f32))
    o_a = forgetting_attention(q_a.reshape(B, S, FOX_HEADS, FOX_HEAD_DIM),
                               k_a.reshape(B, S, FOX_HEADS, FOX_HEAD_DIM),
                               v_a.reshape(B, S, FOX_HEADS, FOX_HEAD_DIM), log_f)

    qr = rotary(q_b.reshape(B, S, RET_HEADS, RET_QK_DIM), positions)
    kr = rotary(k_b.reshape(B, S, RET_HEADS, RET_QK_DIM), positions) * (RET_QK_DIM ** -0.5)
    y = chunkwise_retention(qr, kr, v_b.reshape(B, S, RET_HEADS, RET_V_DIM), retention_decays())
    o_b = jax.nn.silu(g_b) * head_rms_norm(y, ret_norm_g, x.dtype)

    gates = jax.nn.sigmoid(gate_logits).reshape(B, S, N_BRANCH, D_MODEL)
    merged = gates[:, :, 0] * (o_a @ w_branch_fox) + gates[:, :, 1] * (o_b @ w_branch_ret)
    return merged @ w_out


def swiglu(x, w_gate, w_up, w_down):
    return (jax.nn.silu(x @ w_gate) * (x @ w_up)) @ w_down


def moe_swiglu(x, w_router, e_gate, e_up, e_down):
    logits = (x @ w_router).astype(f32)
    top_val, top_idx = lax.top_k(logits, TOP_K)
    top_w = jax.nn.softmax(top_val, axis=-1)
    gate = jnp.sum(jax.nn.one_hot(top_idx, N_EXPERTS, dtype=f32) * top_w[..., None], axis=-2)
    gate = gate.astype(x.dtype)
    out = jnp.zeros_like(x)
    for e in range(N_EXPERTS):
        out = out + gate[..., e:e + 1] * swiglu(x, e_gate[e], e_up[e], e_down[e])
    return out


def setup_inputs(seed: int = 0) -> dict:
    key = jax.random.key(seed)
    ks = jax.random.split(key, 24)
    D = D_MODEL

    def nrm(k, shape, fan_in, gain=1.0):
        return jax.random.normal(k, shape, f32) * (fan_in ** -0.5) * gain

    x = jax.random.normal(ks[0], (BATCH, SEQ, D), f32)
    offsets = jax.random.randint(ks[1], (BATCH, 1), 0, 1024, dtype=jnp.int32)
    positions = (offsets + jnp.arange(SEQ, dtype=jnp.int32)[None, :]).astype(jnp.int32)

    col_scale = jnp.concatenate([
        jnp.ones((2 * FOX_WIDTH,), f32), jnp.full((FOX_WIDTH,), DN_BETA, f32),
        jnp.ones((FOX_HEADS,), f32), jnp.ones((2 * RET_QK_WIDTH,), f32),
        jnp.full((RET_V_WIDTH,), DN_BETA, f32), jnp.ones((RET_V_WIDTH + N_BRANCH * D,), f32)])
    w_in = nrm(ks[2], (DEPTH, D, IN_COLS), D) * col_scale
    b_forget = 1.0 + 3.0 * jax.random.uniform(ks[3], (DEPTH, FOX_HEADS), f32)
    ret_norm_g = 1.0 + 0.02 * jax.random.normal(ks[4], (DEPTH, RET_V_WIDTH), f32)
    w_branch_fox = nrm(ks[5], (DEPTH, FOX_WIDTH, D), FOX_WIDTH, DN_BETA)
    w_branch_ret = nrm(ks[6], (DEPTH, RET_V_WIDTH, D), RET_V_WIDTH, DN_BETA)
    w_out = nrm(ks[7], (DEPTH, D, D), D, DN_BETA)
    ln_mix_g = 1.0 + 0.02 * jax.random.normal(ks[8], (DEPTH, D), f32)
    ln_mix_b = 0.02 * jax.random.normal(ks[9], (DEPTH, D), f32)

    ffn_w_gate = nrm(ks[10], (N_DENSE, D, D_FF), D, DN_BETA)
    ffn_w_up = nrm(ks[11], (N_DENSE, D, D_FF), D, DN_BETA)
    ffn_w_down = nrm(ks[12], (N_DENSE, D_FF, D), D_FF, DN_BETA)

    moe_router = nrm(ks[13], (N_MOE, D, N_EXPERTS), D)
    moe_w_gate = nrm(ks[14], (N_MOE, N_EXPERTS, D, D_FF_EXPERT), D, DN_BETA)
    moe_w_up = nrm(ks[15], (N_MOE, N_EXPERTS, D, D_FF_EXPERT), D, DN_BETA)
    moe_w_down = nrm(ks[16], (N_MOE, N_EXPERTS, D_FF_EXPERT, D), D_FF_EXPERT, DN_BETA)

    ln_ffn_g = 1.0 + 0.02 * jax.random.normal(ks[17], (DEPTH, D), f32)
    ln_ffn_b = 0.02 * jax.random.normal(ks[18], (DEPTH, D), f32)

    return {"x": x, "positions": positions, "w_in": w_in, "b_forget": b_forget,
            "ret_norm_g": ret_norm_g, "w_branch_fox": w_branch_fox, "w_branch_ret": w_branch_ret,
            "w_out": w_out, "ln_mix_g": ln_mix_g, "ln_mix_b": ln_mix_b,
            "ffn_w_gate": ffn_w_gate, "ffn_w_up": ffn_w_up, "ffn_w_down": ffn_w_down,
            "moe_router": moe_router, "moe_w_gate": moe_w_gate, "moe_w_up": moe_w_up,
            "moe_w_down": moe_w_down, "ln_ffn_g": ln_ffn_g, "ln_ffn_b": ln_ffn_b}


def reference(x, positions, w_in, b_forget, ret_norm_g, w_branch_fox, w_branch_ret, w_out,
              ln_mix_g, ln_mix_b, ffn_w_gate, ffn_w_up, ffn_w_down, moe_router, moe_w_gate,
              moe_w_up, moe_w_down, ln_ffn_g, ln_ffn_b):
    for layer in range(DEPTH):
        h = hybrid_mixer(x, positions, w_in[layer], b_forget[layer], ret_norm_g[layer],
                         w_branch_fox[layer], w_branch_ret[layer], w_out[layer])
        x = layer_norm(DN_ALPHA * x + h, ln_mix_g[layer], ln_mix_b[layer])
        i = layer // 2
        if layer % 2 == 0:
            h = swiglu(x, ffn_w_gate[i], ffn_w_up[i], ffn_w_down[i])
        else:
            h = moe_swiglu(x, moe_router[i], moe_w_gate[i], moe_w_up[i], moe_w_down[i])
        x = layer_norm(DN_ALPHA * x + h, ln_ffn_g[layer], ln_ffn_b[layer])
    return x
```

```python
import functools
import math

import numpy as np
import jax
import jax.numpy as jnp
from jax import lax
from jax.experimental import pallas as pl
from jax.experimental.pallas import tpu as pltpu

f32 = jnp.float32
bf16 = jnp.bfloat16

D_MODEL = 1024
DEPTH = 4
FOX_HEADS = 8
FOX_HEAD_DIM = 64
FOX_WIDTH = FOX_HEADS * FOX_HEAD_DIM
RET_HEADS = 8
RET_QK_DIM = 64
RET_V_DIM = 128
RET_QK_WIDTH = RET_HEADS * RET_QK_DIM
RET_V_WIDTH = RET_HEADS * RET_V_DIM
ROPE_BASE = 10000.0
N_BRANCH = 2
D_FF = 2816
N_EXPERTS = 8
TOP_K = 2
D_FF_EXPERT = 3584
DN_ALPHA = (2 * DEPTH) ** 0.25
LN_EPS = 1e-5
RMS_EPS = 1e-6

LANES = 128
SUBLANES = 8
VMEM_LIMIT = 56 * 1024 * 1024

COL_QA = 0
COL_KA = 512
COL_VA = 1024
COL_QB = 1536
COL_VB = 2048
COL_GB = 3072
COL_G0 = 4096
COL_G1 = 5120
COL_KB = 6144
PROJ_COLS = 6656

_O_QA, _O_KA, _O_VA, _O_F, _O_QB, _O_KB, _O_VB, _O_GB, _O_GATE = 0, 512, 1024, 1536, 1544, 2056, 2568, 3592, 4616


def _cparams(sem, vmem=VMEM_LIMIT):
    return pltpu.CompilerParams(dimension_semantics=sem, vmem_limit_bytes=vmem)


def _mm_kernel(x_ref, w_ref, o_ref):
    o_ref[...] = jnp.dot(x_ref[...], w_ref[...], preferred_element_type=f32).astype(o_ref.dtype)


def in_projection(xb, w_all, layer, *, tm, tn):
    M, K = xb.shape
    N = w_all.shape[2]
    return pl.pallas_call(
        _mm_kernel,
        grid=(M // tm, N // tn),
        in_specs=[pl.BlockSpec((tm, K), lambda i, j: (i, 0)),
                  pl.BlockSpec((None, K, tn), lambda i, j: (layer, 0, j))],
        out_specs=pl.BlockSpec((tm, tn), lambda i, j: (i, j)),
        out_shape=jax.ShapeDtypeStruct((M, N), bf16),
        compiler_params=_cparams(("parallel", "arbitrary")),
        name="in_projection",
    )(xb, w_all)


def _forget_cumsum_kernel(x_ref, wf_ref, bf_ref, o_ref, carry_ref, *, ts):
    @pl.when(pl.program_id(1) == 0)
    def _():
        carry_ref[...] = jnp.zeros_like(carry_ref)

    z = lax.dot_general(wf_ref[...], x_ref[...], (((1,), (1,)), ((), ())), preferred_element_type=f32)
    z = z + bf_ref[...]
    lf = jnp.minimum(z, 0.0) - jnp.log1p(jnp.exp(-jnp.abs(z)))
    hi = lf.astype(bf16)
    r1 = lf - hi.astype(f32)
    mid = r1.astype(bf16)
    lo = (r1 - mid.astype(f32)).astype(bf16)
    row = lax.broadcasted_iota(jnp.int32, (ts, ts), 0)
    col = lax.broadcasted_iota(jnp.int32, (ts, ts), 1)
    tri = jnp.where(row <= col, 1.0, 0.0).astype(bf16)
    c = (jnp.dot(hi, tri, preferred_element_type=f32)
         + jnp.dot(mid, tri, preferred_element_type=f32)
         + jnp.dot(lo, tri, preferred_element_type=f32))
    c = c + carry_ref[...]
    o_ref[...] = c[:FOX_HEADS]
    carry_ref[...] = c[:, ts - 1:ts]


def forget_cumsum(xb, wf_all, bf_all, layer, *, B, S, ts):
    K = xb.shape[1]
    nt = S // ts
    return pl.pallas_call(
        functools.partial(_forget_cumsum_kernel, ts=ts),
        grid=(B, nt),
        in_specs=[pl.BlockSpec((ts, K), lambda b, j: (b * nt + j, 0)),
                  pl.BlockSpec((None, 2 * SUBLANES, K), lambda b, j: (layer, 0, 0)),
                  pl.BlockSpec((None, 2 * SUBLANES, 1), lambda b, j: (layer, 0, 0))],
        out_specs=pl.BlockSpec((None, None, FOX_HEADS, ts), lambda b, j: (b, j, 0, 0)),
        out_shape=jax.ShapeDtypeStruct((B, nt, FOX_HEADS, ts), f32),
        scratch_shapes=[pltpu.VMEM((2 * SUBLANES, 1), f32)],
        compiler_params=_cparams(("parallel", "arbitrary")),
        name="forget_cumsum",
    )(xb, wf_all, bf_all)


def _fox_kernel(q_ref, k_ref, v_ref, c_ref, o_ref, m_ref, l_ref, acc_ref, *, tq):
    i = pl.program_id(2)
    lane = lax.broadcasted_iota(jnp.int32, (1, LANES), 1)
    head_mask = (lane < FOX_HEAD_DIM, lane >= FOX_HEAD_DIM)
    q = q_ref[...]
    scale = FOX_HEAD_DIM ** -0.5
    qh = [jnp.where(head_mask[h], q, jnp.zeros_like(q)) * jnp.asarray(scale, q.dtype) for h in range(2)]

    m_ref[...] = jnp.full_like(m_ref, -jnp.inf)
    l_ref[...] = jnp.zeros_like(l_ref)
    acc_ref[...] = jnp.zeros_like(acc_ref)

    row = lax.broadcasted_iota(jnp.int32, (tq, tq), 0)
    col = lax.broadcasted_iota(jnp.int32, (tq, tq), 1)
    causal = col <= row

    def chunk(j, masked):
        k0 = pl.multiple_of(j * tq, tq)
        kc = k_ref[pl.ds(k0, tq), :]
        vc = v_ref[pl.ds(k0, tq), :]
        for h in range(2):
            cr = c_ref[j, h:h + 1, :]
            s = lax.dot_general(qh[h], kc, (((1,), (1,)), ((), ())), preferred_element_type=f32)
            s = s - cr
            if masked:
                s = jnp.where(causal, s, -jnp.inf)
            m_prev = m_ref[h]
            m_new = jnp.maximum(m_prev, jnp.max(s, axis=-1, keepdims=True))
            alpha = jnp.exp(m_prev - m_new)
            p = jnp.exp(s - m_new)
            l_ref[h] = alpha * l_ref[h] + jnp.sum(p, axis=-1, keepdims=True)
            acc_ref[h] = alpha * acc_ref[h] + jnp.dot(p.astype(bf16), vc, preferred_element_type=f32)
            m_ref[h] = m_new

    chunk(i, True)

    def body(j, carry):
        chunk(j, False)
        return carry

    lax.fori_loop(0, i, body, 0)

    o0 = acc_ref[0] / l_ref[0]
    o1 = acc_ref[1] / l_ref[1]
    o_ref[...] = jnp.where(head_mask[0], o0, o1).astype(o_ref.dtype)


def fox_attention(proj, c, *, B, S, tq):
    nq = S // tq
    npair = FOX_HEADS // 2
    c5 = c.reshape(B, nq, npair, 2, tq).transpose(0, 2, 1, 3, 4)
    return pl.pallas_call(
        functools.partial(_fox_kernel, tq=tq),
        grid=(B, npair, nq),
        in_specs=[pl.BlockSpec((tq, LANES), lambda b, p, i: (b * nq + i, COL_QA // LANES + p)),
                  pl.BlockSpec((S, LANES), lambda b, p, i: (b, COL_KA // LANES + p)),
                  pl.BlockSpec((S, LANES), lambda b, p, i: (b, COL_VA // LANES + p)),
                  pl.BlockSpec((None, None, nq, 2, tq), lambda b, p, i: (b, p, 0, 0, 0))],
        out_specs=pl.BlockSpec((tq, LANES), lambda b, p, i: (b * nq + i, p)),
        out_shape=jax.ShapeDtypeStruct((B * S, FOX_WIDTH), bf16),
        scratch_shapes=[pltpu.VMEM((2, tq, 1), f32), pltpu.VMEM((2, tq, 1), f32),
                        pltpu.VMEM((2, tq, LANES), f32)],
        compiler_params=_cparams(("parallel", "parallel", "arbitrary")),
        name="fox_attention",
    )(proj, proj, proj, c5)


def _rope_table_kernel(pos_ref, freq_ref, sign_ref, cos_ref, sin_ref):
    ang = pos_ref[...].astype(f32) * freq_ref[...]
    cos_ref[...] = jnp.cos(ang)
    sin_ref[...] = jnp.sin(ang) * sign_ref[...]


def rope_tables(positions, *, tm):
    n = positions.size
    half = RET_QK_DIM // 2
    inv_freq = ROPE_BASE ** (-jnp.arange(half, dtype=f32) / half)
    freq = jnp.tile(inv_freq, LANES // half).reshape(1, LANES)
    sign = jnp.asarray(np.where(np.arange(LANES) < LANES // 2, -1.0, 1.0), f32).reshape(1, LANES)
    pos = positions.reshape(n, 1)
    out = jax.ShapeDtypeStruct((n, LANES), f32)
    return pl.pallas_call(
        _rope_table_kernel,
        grid=(n // tm,),
        in_specs=[pl.BlockSpec((tm, 1), lambda i: (i, 0)),
                  pl.BlockSpec((1, LANES), lambda i: (0, 0)),
                  pl.BlockSpec((1, LANES), lambda i: (0, 0))],
        out_specs=[pl.BlockSpec((tm, LANES), lambda i: (i, 0))] * 2,
        out_shape=[out, out],
        compiler_params=_cparams(("parallel",)),
        name="rope_tables",
    )(pos, freq, sign)


def _retention_tables(C):
    h = np.arange(RET_HEADS, dtype=np.float64)
    log_gamma = np.log(1.0 - 2.0 ** (-5.0 - h))
    idx = np.arange(C, dtype=np.float64)
    diff = idx[:, None] - idx[None, :]
    scale = RET_QK_DIM ** -0.5
    decay = np.where(diff >= 0, np.exp(np.maximum(diff, 0.0)[None] * log_gamma[:, None, None]), 0.0) * scale
    lane = np.arange(LANES)
    head_lane = (lane % 64) // 32
    masks = np.stack([(head_lane == (hh % 2)) for hh in range(RET_HEADS)]).astype(np.float64)
    q_dec = np.exp((idx + 1.0)[None, :, None] * log_gamma[:, None, None]) * masks[:, None, :]
    k_dec = np.exp((C - 1.0 - idx)[None, :, None] * log_gamma[:, None, None]) * scale * masks[:, None, :]
    chunk_decay = np.exp(C * log_gamma)
    return (jnp.asarray(decay, f32), jnp.asarray(q_dec, f32), jnp.asarray(k_dec, f32),
            [float(v) for v in chunk_decay])


def _retention_kernel(q_ref, k_ref, v_ref, g_ref, cos_ref, sin_ref, dec_ref, qd_ref, kd_ref, ng_ref,
                      o_ref, state_ref, *, chunk_decay):
    @pl.when(pl.program_id(1) == 0)
    def _():
        state_ref[...] = jnp.zeros_like(state_ref)

    cos = cos_ref[...]
    sin = sin_ref[...]
    lane = lax.broadcasted_iota(jnp.int32, (1, LANES), 1)
    own = ((lane % 64) < 32, (lane % 64) >= 32)
    for p in range(RET_HEADS // 2):
        sl = slice(p * LANES, (p + 1) * LANES)
        qp = q_ref[:, sl].astype(f32)
        kp = k_ref[:, sl].astype(f32)
        qr = qp * cos + pltpu.roll(qp, LANES // 2, 1) * sin
        kr = kp * cos + pltpu.roll(kp, LANES // 2, 1) * sin
        for hh in range(2):
            h = 2 * p + hh
            vs = slice(h * RET_V_DIM, (h + 1) * RET_V_DIM)
            qm = jnp.where(own[hh], qr, 0.0).astype(bf16)
            km = jnp.where(own[hh], kr, 0.0).astype(bf16)
            sc = lax.dot_general(qm, km, (((1,), (1,)), ((), ())), preferred_element_type=f32)
            sc = (sc * dec_ref[h]).astype(bf16)
            v = v_ref[:, vs]
            intra = jnp.dot(sc, v, preferred_element_type=f32)
            st = state_ref[h]
            qd = (qr * qd_ref[h]).astype(bf16)
            cross = jnp.dot(qd, st.astype(bf16), preferred_element_type=f32)
            y = intra + cross
            kd = (kr * kd_ref[h]).astype(bf16)
            state_ref[h] = chunk_decay[h] * st + lax.dot_general(
                kd, v, (((0,), (0,)), ((), ())), preferred_element_type=f32)
            ms = jnp.mean(y * y, axis=-1, keepdims=True)
            yn = y * lax.rsqrt(ms + RMS_EPS) * ng_ref[:, vs]
            g = g_ref[:, vs].astype(f32)
            o_ref[:, vs] = (g * jax.nn.sigmoid(g) * yn).astype(o_ref.dtype)


def retention(proj, cos_t, sin_t, ng_all, layer, *, B, S, C):
    nc = S // C
    dec, q_dec, k_dec, chunk_decay = _retention_tables(C)
    row = lambda b, c: b * nc + c
    const3 = lambda b, c: (0, 0, 0)
    return pl.pallas_call(
        functools.partial(_retention_kernel, chunk_decay=chunk_decay),
        grid=(B, nc),
        in_specs=[pl.BlockSpec((C, RET_QK_WIDTH), lambda b, c: (row(b, c), COL_QB // RET_QK_WIDTH)),
                  pl.BlockSpec((C, RET_QK_WIDTH), lambda b, c: (row(b, c), COL_KB // RET_QK_WIDTH)),
                  pl.BlockSpec((C, RET_V_WIDTH), lambda b, c: (row(b, c), COL_VB // RET_V_WIDTH)),
                  pl.BlockSpec((C, RET_V_WIDTH), lambda b, c: (row(b, c), COL_GB // RET_V_WIDTH)),
                  pl.BlockSpec((C, LANES), lambda b, c: (row(b, c), 0)),
                  pl.BlockSpec((C, LANES), lambda b, c: (row(b, c), 0)),
                  pl.BlockSpec((RET_HEADS, C, C), const3),
                  pl.BlockSpec((RET_HEADS, C, LANES), const3),
                  pl.BlockSpec((RET_HEADS, C, LANES), const3),
                  pl.BlockSpec((None, 1, RET_V_WIDTH), lambda b, c: (layer, 0, 0))],
        out_specs=pl.BlockSpec((C, RET_V_WIDTH), lambda b, c: (row(b, c), 0)),
        out_shape=jax.ShapeDtypeStruct((B * S, RET_V_WIDTH), bf16),
        scratch_shapes=[pltpu.VMEM((RET_HEADS, LANES, RET_V_DIM), f32)],
        compiler_params=_cparams(("parallel", "arbitrary")),
        name="retention",
    )(proj, proj, proj, proj, cos_t, sin_t, dec, q_dec, k_dec, ng_all)


def _layer_norm(y, g, b):
    mu = jnp.mean(y, axis=-1, keepdims=True)
    d = y - mu
    var = jnp.mean(d * d, axis=-1, keepdims=True)
    return d * lax.rsqrt(var + LN_EPS) * g + b


def _top2(logits):
    lane = lax.broadcasted_iota(jnp.int32, logits.shape, 1).astype(f32)
    big = float(LANES)
    lg = jnp.where(lane < N_EXPERTS, logits, -jnp.inf)
    m1 = jnp.max(lg, axis=-1, keepdims=True)
    e1 = jnp.min(jnp.where(lg == m1, lane, big), axis=-1, keepdims=True)
    lg2 = jnp.where(lane == e1, -jnp.inf, lg)
    m2 = jnp.max(lg2, axis=-1, keepdims=True)
    e2 = jnp.min(jnp.where(lg2 == m2, lane, big), axis=-1, keepdims=True)
    ex = jnp.exp(m2 - m1)
    den = 1.0 + ex
    return e1, e2, 1.0 / den, ex / den


def _merge_kernel(*refs, route):
    if route:
        (oa_ref, ob_ref, g0_ref, g1_ref, x_ref, wf_ref, wr_ref, wo_ref, lg_ref, lb_ref, wrt_ref,
         xo_ref, xb_ref, idx_ref, wts_ref) = refs
    else:
        (oa_ref, ob_ref, g0_ref, g1_ref, x_ref, wf_ref, wr_ref, wo_ref, lg_ref, lb_ref,
         xo_ref, xb_ref) = refs
    a = jnp.dot(oa_ref[...], wf_ref[...], preferred_element_type=f32)
    b = jnp.dot(ob_ref[...], wr_ref[...], preferred_element_type=f32)
    s0 = jax.nn.sigmoid(g0_ref[...].astype(f32))
    s1 = jax.nn.sigmoid(g1_ref[...].astype(f32))
    merged = (s0 * a + s1 * b).astype(bf16)
    h = jnp.dot(merged, wo_ref[...], preferred_element_type=f32)
    xn = _layer_norm(DN_ALPHA * x_ref[...] + h, lg_ref[...], lb_ref[...])
    xo_ref[...] = xn
    xnb = xn.astype(bf16)
    xb_ref[...] = xnb
    if route:
        logits = jnp.dot(xnb, wrt_ref[...], preferred_element_type=f32)
        e1, e2, w1, w2 = _top2(logits)
        lane = lax.broadcasted_iota(jnp.int32, logits.shape, 1)
        idx_ref[...] = jnp.where(lane == 0, e1, jnp.where(lane == 1, e2, 0.0)).astype(jnp.int32)
        wts_ref[...] = jnp.where(lane == 0, w1, jnp.where(lane == 1, w2, 0.0))


def merge_out(o_a, o_b, proj, x, wfox_all, wret_all, wout_all, lng_all, lnb_all, layer, *, tm,
              router_all=None, router_idx=0):
    M = x.shape[0]
    route = router_all is not None
    lw = lambda i: (layer, 0, 0)
    in_specs = [pl.BlockSpec((tm, FOX_WIDTH), lambda i: (i, 0)),
                pl.BlockSpec((tm, RET_V_WIDTH), lambda i: (i, 0)),
                pl.BlockSpec((tm, D_MODEL), lambda i: (i, COL_G0 // D_MODEL)),
                pl.BlockSpec((tm, D_MODEL), lambda i: (i, COL_G1 // D_MODEL)),
                pl.BlockSpec((tm, D_MODEL), lambda i: (i, 0)),
                pl.BlockSpec((None, FOX_WIDTH, D_MODEL), lw),
                pl.BlockSpec((None, RET_V_WIDTH, D_MODEL), lw),
                pl.BlockSpec((None, D_MODEL, D_MODEL), lw),
                pl.BlockSpec((None, 1, D_MODEL), lw),
                pl.BlockSpec((None, 1, D_MODEL), lw)]
    args = [o_a, o_b, proj, proj, x, wfox_all, wret_all, wout_all, lng_all, lnb_all]
    out_specs = [pl.BlockSpec((tm, D_MODEL), lambda i: (i, 0)), pl.BlockSpec((tm, D_MODEL), lambda i: (i, 0))]
    out_shape = [jax.ShapeDtypeStruct((M, D_MODEL), f32), jax.ShapeDtypeStruct((M, D_MODEL), bf16)]
    if route:
        in_specs.append(pl.BlockSpec((None, D_MODEL, LANES), lambda i: (router_idx, 0, 0)))
        args.append(router_all)
        out_specs += [pl.BlockSpec((tm, LANES), lambda i: (i, 0))] * 2
        out_shape += [jax.ShapeDtypeStruct((M, LANES), jnp.int32), jax.ShapeDtypeStruct((M, LANES), f32)]
    return pl.pallas_call(
        functools.partial(_merge_kernel, route=route),
        grid=(M // tm,),
        in_specs=in_specs,
        out_specs=out_specs,
        out_shape=out_shape,
        compiler_params=_cparams(("parallel",)),
        name="merge_out",
    )(*args)


def _swiglu_partial(x, wg, wu, wd):
    g = jnp.dot(x, wg, preferred_element_type=f32)
    u = jnp.dot(x, wu, preferred_element_type=f32)
    h = (g * jax.nn.sigmoid(g) * u).astype(bf16)
    return jnp.dot(h, wd, preferred_element_type=f32)


def _dense_ffn_kernel(xb_ref, x_ref, wg_ref, wu_ref, wd_ref, lg_ref, lb_ref, xo_ref, xbo_ref, acc_ref):
    f = pl.program_id(1)
    part = _swiglu_partial(xb_ref[...], wg_ref[...], wu_ref[...], wd_ref[...])

    @pl.when(f == 0)
    def _():
        acc_ref[...] = part

    @pl.when(f > 0)
    def _():
        acc_ref[...] += part

    @pl.when(f == pl.num_programs(1) - 1)
    def _():
        xn = _layer_norm(DN_ALPHA * x_ref[...] + acc_ref[...], lg_ref[...], lb_ref[...])
        xo_ref[...] = xn
        xbo_ref[...] = xn.astype(bf16)


def dense_ffn(xb, x, wg_all, wu_all, wd_all, lng_all, lnb_all, layer, widx, *, tm, tf):
    M = x.shape[0]
    F = wg_all.shape[2]
    return pl.pallas_call(
        _dense_ffn_kernel,
        grid=(M // tm, F // tf),
        in_specs=[pl.BlockSpec((tm, D_MODEL), lambda i, f: (i, 0)),
                  pl.BlockSpec((tm, D_MODEL), lambda i, f: (i, 0)),
                  pl.BlockSpec((None, D_MODEL, tf), lambda i, f: (widx, 0, f)),
                  pl.BlockSpec((None, D_MODEL, tf), lambda i, f: (widx, 0, f)),
                  pl.BlockSpec((None, tf, D_MODEL), lambda i, f: (widx, f, 0)),
                  pl.BlockSpec((None, 1, D_MODEL), lambda i, f: (layer, 0, 0)),
                  pl.BlockSpec((None, 1, D_MODEL), lambda i, f: (layer, 0, 0))],
        out_specs=[pl.BlockSpec((tm, D_MODEL), lambda i, f: (i, 0)),
                   pl.BlockSpec((tm, D_MODEL), lambda i, f: (i, 0))],
        out_shape=[jax.ShapeDtypeStruct((M, D_MODEL), f32), jax.ShapeDtypeStruct((M, D_MODEL), bf16)],
        scratch_shapes=[pltpu.VMEM((tm, D_MODEL), f32)],
        compiler_params=_cparams(("parallel", "arbitrary")),
        name="dense_ffn",
    )(xb, x, wg_all, wu_all, wd_all, lng_all, lnb_all)


def _row_slice(r):
    return pl.ds(pl.multiple_of(r * SUBLANES, SUBLANES), SUBLANES)


def _rows_to_tiles(dst_ref, val):
    tm = val.shape[0]
    for c in range(SUBLANES):
        dst_ref[pl.ds(c, tm, stride=SUBLANES), :] = val[:, c * LANES:(c + 1) * LANES]


def _tiles_to_rows(src_ref):
    tm = src_ref.shape[0] // SUBLANES
    return jnp.concatenate([src_ref[pl.ds(c, tm, stride=SUBLANES), :] for c in range(SUBLANES)], axis=-1)


def _dispatch_kernel(x_ref, pos_hbm, xs_in_hbm, xs_hbm, buf_ref, pos_smem, sem_pos, sem_rows, *, tm):
    del xs_in_hbm
    i = pl.program_id(0)
    cp = pltpu.make_async_copy(pos_hbm.at[i], pos_smem, sem_pos)
    cp.start()
    _rows_to_tiles(buf_ref, x_ref[...])
    cp.wait()

    def row_copy(t, k):
        return pltpu.make_async_copy(buf_ref.at[_row_slice(t), :],
                                     xs_hbm.at[_row_slice(pos_smem[TOP_K * t + k]), :], sem_rows)

    def issue(t, carry):
        for k in range(TOP_K):
            row_copy(t, k).start()
        return carry

    lax.fori_loop(0, tm, issue, 0)

    def drain(t, carry):
        for k in range(TOP_K):
            row_copy(t, k).wait()
        return carry

    lax.fori_loop(0, tm, drain, 0)


def moe_dispatch(x, pos_tiles, n_rows, *, tm):
    M = x.shape[0]
    zeros = jnp.zeros((n_rows * SUBLANES, LANES), f32)
    return pl.pallas_call(
        functools.partial(_dispatch_kernel, tm=tm),
        grid=(M // tm,),
        in_specs=[pl.BlockSpec((tm, D_MODEL), lambda i: (i, 0)),
                  pl.BlockSpec(memory_space=pl.ANY),
                  pl.BlockSpec(memory_space=pl.ANY)],
        out_specs=pl.BlockSpec(memory_space=pl.ANY),
        out_shape=jax.ShapeDtypeStruct((n_rows * SUBLANES, LANES), f32),
        scratch_shapes=[pltpu.VMEM((tm * SUBLANES, LANES), f32),
                        pltpu.SMEM((TOP_K * tm,), jnp.int32),
                        pltpu.SemaphoreType.DMA(()),
                        pltpu.SemaphoreType.DMA(())],
        input_output_aliases={2: 0},
        compiler_params=_cparams(("arbitrary",)),
        name="moe_dispatch",
    )(x, pos_tiles, zeros)


def _moe_ffn_kernel(te_ref, xs_ref, wg_ref, wu_ref, wd_ref, ys_ref, xb_ref, acc_ref, *, n_tiles):
    i = pl.program_id(0)
    f = pl.program_id(1)
    valid = te_ref[n_tiles + i] == 1

    @pl.when(valid)
    def _():
        @pl.when(f == 0)
        def _():
            xb_ref[...] = _tiles_to_rows(xs_ref).astype(bf16)

        part = _swiglu_partial(xb_ref[...], wg_ref[...], wu_ref[...], wd_ref[...])

        @pl.when(f == 0)
        def _():
            acc_ref[...] = part

        @pl.when(f > 0)
        def _():
            acc_ref[...] += part

    @pl.when(f == pl.num_programs(1) - 1)
    def _():
        @pl.when(valid)
        def _():
            _rows_to_tiles(ys_ref, acc_ref[...])

        @pl.when(jnp.logical_not(valid))
        def _():
            ys_ref[...] = jnp.zeros_like(ys_ref)


def moe_ffn(tile_info, xs, wg_all, wu_all, wd_all, widx, *, tm, tf):
    R = xs.shape[0] // SUBLANES
    n_tiles = R // tm
    F = wg_all.shape[3]
    grid_spec = pltpu.PrefetchScalarGridSpec(
        num_scalar_prefetch=1,
        grid=(n_tiles, F // tf),
        in_specs=[pl.BlockSpec((tm * SUBLANES, LANES), lambda i, f, te: (i, 0)),
                  pl.BlockSpec((None, None, D_MODEL, tf), lambda i, f, te: (widx, te[i], 0, f)),
                  pl.BlockSpec((None, None, D_MODEL, tf), lambda i, f, te: (widx, te[i], 0, f)),
                  pl.BlockSpec((None, None, tf, D_MODEL), lambda i, f, te: (widx, te[i], f, 0))],
        out_specs=pl.BlockSpec((tm * SUBLANES, LANES), lambda i, f, te: (i, 0)),
        scratch_shapes=[pltpu.VMEM((tm, D_MODEL), bf16), pltpu.VMEM((tm, D_MODEL), f32)],
    )
    return pl.pallas_call(
        functools.partial(_moe_ffn_kernel, n_tiles=n_tiles),
        grid_spec=grid_spec,
        out_shape=jax.ShapeDtypeStruct((R * SUBLANES, LANES), f32),
        compiler_params=_cparams(("arbitrary", "arbitrary")),
        name="moe_ffn",
    )(tile_info, xs, wg_all, wu_all, wd_all)


def _combine_kernel(x_ref, wts_ref, lg_ref, lb_ref, pos_hbm, ys_hbm, xo_ref, xbo_ref,
                    buf_ref, pos_smem, sem_pos, sem_rows, *, tm):
    i = pl.program_id(0)
    cp = pltpu.make_async_copy(pos_hbm.at[i], pos_smem, sem_pos)
    cp.start()
    cp.wait()

    def row_copy(t, k):
        return pltpu.make_async_copy(ys_hbm.at[_row_slice(pos_smem[TOP_K * t + k]), :],
                                     buf_ref.at[k, _row_slice(t), :], sem_rows)

    def issue(t, carry):
        for k in range(TOP_K):
            row_copy(t, k).start()
        return carry

    lax.fori_loop(0, tm, issue, 0)

    def drain(t, carry):
        for k in range(TOP_K):
            row_copy(t, k).wait()
        return carry

    lax.fori_loop(0, tm, drain, 0)

    wts = wts_ref[...]
    h = wts[:, 0:1] * _tiles_to_rows(buf_ref.at[0]) + wts[:, 1:2] * _tiles_to_rows(buf_ref.at[1])
    xn = _layer_norm(DN_ALPHA * x_ref[...] + h, lg_ref[...], lb_ref[...])
    xo_ref[...] = xn
    xbo_ref[...] = xn.astype(bf16)


def moe_combine(x, wts, ys, pos_tiles, lng_all, lnb_all, layer, *, tm):
    M = x.shape[0]
    return pl.pallas_call(
        functools.partial(_combine_kernel, tm=tm),
        grid=(M // tm,),
        in_specs=[pl.BlockSpec((tm, D_MODEL), lambda i: (i, 0)),
                  pl.BlockSpec((tm, LANES), lambda i: (i, 0)),
                  pl.BlockSpec((None, 1, D_MODEL), lambda i: (layer, 0, 0)),
                  pl.BlockSpec((None, 1, D_MODEL), lambda i: (layer, 0, 0)),
                  pl.BlockSpec(memory_space=pl.ANY),
                  pl.BlockSpec(memory_space=pl.ANY)],
        out_specs=[pl.BlockSpec((tm, D_MODEL), lambda i: (i, 0)),
                   pl.BlockSpec((tm, D_MODEL), lambda i: (i, 0))],
        out_shape=[jax.ShapeDtypeStruct((M, D_MODEL), f32), jax.ShapeDtypeStruct((M, D_MODEL), bf16)],
        scratch_shapes=[pltpu.VMEM((TOP_K, tm * SUBLANES, LANES), f32),
                        pltpu.SMEM((TOP_K * tm,), jnp.int32),
                        pltpu.SemaphoreType.DMA(()),
                        pltpu.SemaphoreType.DMA(())],
        compiler_params=_cparams(("arbitrary",)),
        name="moe_combine",
    )(x, wts, lng_all, lnb_all, pos_tiles, ys)


def _routing_plan(idx, *, tm_rows, n_tiles):
    e = idx[:, :TOP_K].reshape(-1)
    onehot = (e[:, None] == jnp.arange(N_EXPERTS, dtype=jnp.int32)[None, :]).astype(jnp.int32)
    csum = jnp.cumsum(onehot, axis=0)
    counts = csum[-1]
    rank = jnp.take_along_axis(csum - onehot, e[:, None], axis=1)[:, 0]
    tiles_e = (counts + tm_rows - 1) // tm_rows
    tile_end = jnp.cumsum(tiles_e)
    row_start = (tile_end - tiles_e) * tm_rows
    pos = row_start[e] + rank
    t = jnp.arange(n_tiles, dtype=jnp.int32)
    te = jnp.sum((t[:, None] >= tile_end[None, :]).astype(jnp.int32), axis=1)
    valid = (t < tile_end[-1]).astype(jnp.int32)
    last_e = jnp.max(jnp.where(tiles_e > 0, jnp.arange(N_EXPERTS, dtype=jnp.int32), 0))
    te = jnp.where(valid == 1, jnp.minimum(te, N_EXPERTS - 1), last_e)
    return pos.astype(jnp.int32), jnp.concatenate([te, valid]).astype(jnp.int32)


def _config(B, S):
    M = B * S
    tm_moe = min(1024, M)
    return dict(
        proj_tm=min(2048, M), proj_tn=512,
        fc_ts=min(512, S),
        fox_tq=min(512, S),
        ret_c=min(256, S),
        rope_tm=min(2048, M),
        merge_tm=min(512, M),
        ffn_tm=min(512, M), ffn_tf=1408,
        moe_tm=tm_moe, moe_tf=896,
        moe_tiles=(M * TOP_K) // tm_moe + N_EXPERTS,
        disp_tm=min(512, M), comb_tm=min(256, M),
    )


def _permute_rotary_cols(w):
    lead = w.shape[:-1]
    n = len(lead)
    w = w.reshape(lead + (RET_HEADS // 2, 2, 2, RET_QK_DIM // 2))
    w = jnp.swapaxes(w, n + 1, n + 2)
    return w.reshape(lead + (RET_QK_WIDTH,))


def _prep_in_weights(w_in, b_forget):
    w_main = jnp.concatenate([
        w_in[:, :, _O_QA:_O_F],
        _permute_rotary_cols(w_in[:, :, _O_QB:_O_KB]),
        w_in[:, :, _O_VB:],
        _permute_rotary_cols(w_in[:, :, _O_KB:_O_VB]),
    ], axis=-1).astype(bf16)
    pad_h = 2 * SUBLANES - FOX_HEADS
    w_f = jnp.pad(jnp.swapaxes(w_in[:, :, _O_F:_O_QB], 1, 2), ((0, 0), (0, pad_h), (0, 0))).astype(bf16)
    b_f = jnp.pad(b_forget, ((0, 0), (0, pad_h)))[:, :, None]
    return w_main, w_f, b_f


def kernel(x, positions, w_in, b_forget, ret_norm_g, w_branch_fox, w_branch_ret, w_out, ln_mix_g, ln_mix_b,
           ffn_w_gate, ffn_w_up, ffn_w_down, moe_router, moe_w_gate, moe_w_up, moe_w_down, ln_ffn_g, ln_ffn_b):
    B, S, _ = x.shape
    M = B * S
    cfg = _config(B, S)

    w_main, w_f, b_f = _prep_in_weights(w_in, b_forget)
    ng = ret_norm_g[:, None, :]
    wfox = w_branch_fox.astype(bf16)
    wret = w_branch_ret.astype(bf16)
    wout = w_out.astype(bf16)
    lmg, lmb = ln_mix_g[:, None, :], ln_mix_b[:, None, :]
    lfg, lfb = ln_ffn_g[:, None, :], ln_ffn_b[:, None, :]
    fwg, fwu, fwd = ffn_w_gate.astype(bf16), ffn_w_up.astype(bf16), ffn_w_down.astype(bf16)
    mwg, mwu, mwd = moe_w_gate.astype(bf16), moe_w_up.astype(bf16), moe_w_down.astype(bf16)
    wrt = jnp.pad(moe_router, ((0, 0), (0, 0), (0, LANES - N_EXPERTS))).astype(bf16)

    cos_t, sin_t = rope_tables(positions, tm=cfg["rope_tm"])

    xf = x.reshape(M, D_MODEL)
    xb = xf.astype(bf16)
    for layer in range(DEPTH):
        proj = in_projection(xb, w_main, layer, tm=cfg["proj_tm"], tn=cfg["proj_tn"])
        c = forget_cumsum(xb, w_f, b_f, layer, B=B, S=S, ts=cfg["fc_ts"])
        o_a = fox_attention(proj, c, B=B, S=S, tq=cfg["fox_tq"])
        o_b = retention(proj, cos_t, sin_t, ng, layer, B=B, S=S, C=cfg["ret_c"])
        i = layer // 2
        if layer % 2 == 0:
            xf, xb = merge_out(o_a, o_b, proj, xf, wfox, wret, wout, lmg, lmb, layer, tm=cfg["merge_tm"])
            xf, xb = dense_ffn(xb, xf, fwg, fwu, fwd, lfg, lfb, layer, i, tm=cfg["ffn_tm"], tf=cfg["ffn_tf"])
        else:
            xf, _, idx, wts = merge_out(o_a, o_b, proj, xf, wfox, wret, wout, lmg, lmb, layer,
                                        tm=cfg["merge_tm"], router_all=wrt, router_idx=i)
            n_tiles = cfg["moe_tiles"]
            pos, tile_info = _routing_plan(idx, tm_rows=cfg["moe_tm"], n_tiles=n_tiles)
            xs = moe_dispatch(xf, pos.reshape(M // cfg["disp_tm"], TOP_K * cfg["disp_tm"]),
                              n_tiles * cfg["moe_tm"], tm=cfg["disp_tm"])
            ys = moe_ffn(tile_info, xs, mwg, mwu, mwd, i, tm=cfg["moe_tm"], tf=cfg["moe_tf"])
            xf, xb = moe_combine(xf, wts, ys, pos.reshape(M // cfg["comb_tm"], TOP_K * cfg["comb_tm"]),
                                 lfg, lfb, layer, tm=cfg["comb_tm"])
    return xf.reshape(B, S, D_MODEL)
```

```python
import functools
import math

import numpy as np
import jax
import jax.numpy as jnp
from jax import lax
from jax.experimental import pallas as pl
from jax.experimental.pallas import tpu as pltpu

f32 = jnp.float32
bf16 = jnp.bfloat16

D_MODEL = 1024
DEPTH = 4
FOX_HEADS = 8
FOX_HEAD_DIM = 64
FOX_WIDTH = FOX_HEADS * FOX_HEAD_DIM
RET_HEADS = 8
RET_QK_DIM = 64
RET_V_DIM = 128
RET_QK_WIDTH = RET_HEADS * RET_QK_DIM
RET_V_WIDTH = RET_HEADS * RET_V_DIM
ROPE_BASE = 10000.0
N_BRANCH = 2
D_FF = 2816
N_EXPERTS = 8
TOP_K = 2
D_FF_EXPERT = 3584
DN_ALPHA = (2 * DEPTH) ** 0.25
LN_EPS = 1e-5
RMS_EPS = 1e-6

LANES = 128
SUBLANES = 8
VMEM_LIMIT = 56 * 1024 * 1024

COL_QA = 0
COL_KA = 512
COL_VA = 1024
COL_QB = 1536
COL_VB = 2048
COL_GB = 3072
COL_G0 = 4096
COL_G1 = 5120
COL_KB = 6144
PROJ_COLS = 6656

_O_QA, _O_KA, _O_VA, _O_F, _O_QB, _O_KB, _O_VB, _O_GB, _O_GATE = 0, 512, 1024, 1536, 1544, 2056, 2568, 3592, 4616


def _cparams(sem, vmem=VMEM_LIMIT):
    return pltpu.CompilerParams(dimension_semantics=sem, vmem_limit_bytes=vmem)


def _mm_kernel(x_ref, w_ref, o_ref):
    o_ref[...] = jnp.dot(x_ref[...], w_ref[...], preferred_element_type=f32).astype(o_ref.dtype)


def in_projection(xb, w_all, layer, *, tm, tn):
    M, K = xb.shape
    N = w_all.shape[2]
    return pl.pallas_call(
        _mm_kernel,
        grid=(M // tm, N // tn),
        in_specs=[pl.BlockSpec((tm, K), lambda i, j: (i, 0)),
                  pl.BlockSpec((None, K, tn), lambda i, j: (layer, 0, j))],
        out_specs=pl.BlockSpec((tm, tn), lambda i, j: (i, j)),
        out_shape=jax.ShapeDtypeStruct((M, N), bf16),
        compiler_params=_cparams(("parallel", "arbitrary")),
        name="in_projection",
    )(xb, w_all)


C_TERMS = 3
LOG2E = math.log2(math.e)


def _c_lane(h):
    return (FOX_HEAD_DIM if h % 2 == 0 else 0) + C_TERMS * (h // 2)


def _split_bf16(v):
    hi = v.astype(bf16)
    r1 = v - hi.astype(f32)
    mid = r1.astype(bf16)
    lo = (r1 - mid.astype(f32)).astype(bf16)
    return hi, mid, lo


def _forget_cumsum_kernel(x_ref, wf_ref, bf_ref, term_ref, o_ref, carry_ref, *, ts):
    @pl.when(pl.program_id(1) == 0)
    def _():
        carry_ref[...] = jnp.zeros_like(carry_ref)

    term = term_ref[...]
    z = jnp.dot(x_ref[...], wf_ref[...], preferred_element_type=f32) + bf_ref[...]
    lf = jnp.minimum(z, 0.0) - jnp.log1p(jnp.exp(-jnp.abs(z)))
    lf = jnp.where(term >= 0, lf, 0.0)
    row = lax.broadcasted_iota(jnp.int32, (ts, ts), 0)
    col = lax.broadcasted_iota(jnp.int32, (ts, ts), 1)
    tri = jnp.where(col <= row, 1.0, 0.0).astype(bf16)
    c = carry_ref[...]
    for part in _split_bf16(lf):
        c = c + jnp.dot(tri, part, preferred_element_type=f32)
    carry_ref[...] = c[ts - 1:ts, :]
    hi, mid, lo = _split_bf16(c * LOG2E)
    o_ref[...] = jnp.where(term == 0, hi, jnp.where(term == 1, mid, jnp.where(term == 2, lo, jnp.zeros_like(hi))))


def forget_cumsum(xb, wf_all, bf_all, layer, *, B, S, ts):
    K = xb.shape[1]
    nt = S // ts
    term = np.full((1, LANES), -1, np.int32)
    for h in range(FOX_HEADS):
        term[0, _c_lane(h):_c_lane(h) + C_TERMS] = np.arange(C_TERMS)
    return pl.pallas_call(
        functools.partial(_forget_cumsum_kernel, ts=ts),
        grid=(B, nt),
        in_specs=[pl.BlockSpec((ts, K), lambda b, j: (b * nt + j, 0)),
                  pl.BlockSpec((None, K, LANES), lambda b, j: (layer, 0, 0)),
                  pl.BlockSpec((None, 1, LANES), lambda b, j: (layer, 0, 0)),
                  pl.BlockSpec((1, LANES), lambda b, j: (0, 0))],
        out_specs=pl.BlockSpec((ts, LANES), lambda b, j: (b * nt + j, 0)),
        out_shape=jax.ShapeDtypeStruct((B * S, LANES), bf16),
        scratch_shapes=[pltpu.VMEM((1, LANES), f32)],
        compiler_params=_cparams(("parallel", "arbitrary")),
        name="forget_cumsum",
    )(xb, wf_all, bf_all, jnp.asarray(term))


def _fox_kernel(q_ref, k_ref, v_ref, ca_ref, o_ref, m_ref, acc_ref, *, tq):
    pair = pl.program_id(1)
    i = pl.program_id(2)
    lane = lax.broadcasted_iota(jnp.int32, (1, LANES), 1)
    head_lanes = (lane < FOX_HEAD_DIM, lane >= FOX_HEAD_DIM)
    q = q_ref[...] * jnp.asarray(FOX_HEAD_DIM ** -0.5 * LOG2E, bf16)
    qh = []
    for h in range(2):
        first = (FOX_HEAD_DIM if h == 0 else 0) + C_TERMS * pair
        minus_one = jnp.where((lane >= first) & (lane < first + C_TERMS), -1.0, 0.0).astype(bf16)
        qh.append(jnp.where(head_lanes[h], q, minus_one))

    m_ref[...] = jnp.full_like(m_ref, -jnp.inf)
    acc_ref[...] = jnp.zeros_like(acc_ref)

    key = lax.broadcasted_iota(jnp.int32, (tq, tq), 0)
    qry = lax.broadcasted_iota(jnp.int32, (tq, tq), 1)
    causal = key <= qry

    def chunks(js, masked):
        sts, vcs = [], []
        for j in js:
            k0 = pl.multiple_of(j * tq, tq)
            kc = k_ref[pl.ds(k0, tq), :]
            vcs.append(v_ref[pl.ds(k0, tq), :])
            ca = ca_ref[pl.ds(k0, tq), :]
            for h in range(2):
                kh = jnp.where(head_lanes[h], kc, ca)
                st = lax.dot_general(kh, qh[h], (((1,), (1,)), ((), ())), preferred_element_type=f32)
                if masked:
                    st = jnp.where(causal, st, -jnp.inf)
                sts.append(st)
        for n in range(len(js)):
            for h in range(2):
                st = sts[2 * n + h]
                m_prev = m_ref[h]
                m_new = jnp.maximum(m_prev, jnp.max(st, axis=0, keepdims=True))
                alpha = jnp.exp2(m_prev - m_new)
                p = jnp.exp2(st - m_new).astype(bf16)
                vh = jnp.where(head_lanes[h], vcs[n], jnp.ones_like(vcs[n]))
                pv = lax.dot_general(vh, p, (((0,), (0,)), ((), ())), preferred_element_type=f32)
                acc_ref[h] = alpha * acc_ref[h] + pv
                m_ref[h] = m_new

    chunks([i], True)

    @pl.when(i % 2 == 1)
    def _():
        chunks([i - 1], False)

    def body(jj, carry):
        chunks([2 * jj, 2 * jj + 1], False)
        return carry

    lax.fori_loop(0, i // 2, body, 0)

    a0, a1 = acc_ref[0], acc_ref[1]
    o0 = a0 / a0[FOX_HEAD_DIM:FOX_HEAD_DIM + 1, :]
    o1 = a1 / a1[0:1, :]
    sub = lax.broadcasted_iota(jnp.int32, (LANES, 1), 0)
    ot = jnp.where(sub < FOX_HEAD_DIM, o0, o1)
    o_ref[...] = ot.T.astype(o_ref.dtype)


def fox_attention(proj, caug, *, B, S, tq):
    nq = S // tq
    npair = FOX_HEADS // 2
    return pl.pallas_call(
        functools.partial(_fox_kernel, tq=tq),
        grid=(B, npair, nq),
        in_specs=[pl.BlockSpec((tq, LANES), lambda b, p, i: (b * nq + i, COL_QA // LANES + p)),
                  pl.BlockSpec((S, LANES), lambda b, p, i: (b, COL_KA // LANES + p)),
                  pl.BlockSpec((S, LANES), lambda b, p, i: (b, COL_VA // LANES + p)),
                  pl.BlockSpec((S, LANES), lambda b, p, i: (b, 0))],
        out_specs=pl.BlockSpec((tq, LANES), lambda b, p, i: (b * nq + i, p)),
        out_shape=jax.ShapeDtypeStruct((B * S, FOX_WIDTH), bf16),
        scratch_shapes=[pltpu.VMEM((2, 1, tq), f32), pltpu.VMEM((2, LANES, tq), f32)],
        compiler_params=_cparams(("parallel", "parallel", "arbitrary")),
        name="fox_attention",
    )(proj, proj, proj, caug)


def _rope_table_kernel(pos_ref, freq_ref, sign_ref, cos_ref, sin_ref):
    ang = pos_ref[...].astype(f32) * freq_ref[...]
    cos_ref[...] = jnp.cos(ang)
    sin_ref[...] = jnp.sin(ang) * sign_ref[...]


def rope_tables(positions, *, tm):
    n = positions.size
    half = RET_QK_DIM // 2
    inv_freq = ROPE_BASE ** (-jnp.arange(half, dtype=f32) / half)
    freq = jnp.tile(inv_freq, LANES // half).reshape(1, LANES)
    sign = jnp.asarray(np.where(np.arange(LANES) < LANES // 2, -1.0, 1.0), f32).reshape(1, LANES)
    pos = positions.reshape(n, 1)
    out = jax.ShapeDtypeStruct((n, LANES), f32)
    return pl.pallas_call(
        _rope_table_kernel,
        grid=(n // tm,),
        in_specs=[pl.BlockSpec((tm, 1), lambda i: (i, 0)),
                  pl.BlockSpec((1, LANES), lambda i: (0, 0)),
                  pl.BlockSpec((1, LANES), lambda i: (0, 0))],
        out_specs=[pl.BlockSpec((tm, LANES), lambda i: (i, 0))] * 2,
        out_shape=[out, out],
        compiler_params=_cparams(("parallel",)),
        name="rope_tables",
    )(pos, freq, sign)


def _retention_tables(C):
    h = np.arange(RET_HEADS, dtype=np.float64)
    log_gamma = np.log(1.0 - 2.0 ** (-5.0 - h))
    idx = np.arange(C, dtype=np.float64)
    diff = idx[:, None] - idx[None, :]
    scale = RET_QK_DIM ** -0.5
    decay = np.where(diff >= 0, np.exp(np.maximum(diff, 0.0)[None] * log_gamma[:, None, None]), 0.0) * scale
    lane = np.arange(LANES)
    head_lane = (lane % 64) // 32
    masks = np.stack([(head_lane == (hh % 2)) for hh in range(RET_HEADS)]).astype(np.float64)
    q_dec = np.exp((idx + 1.0)[None, :, None] * log_gamma[:, None, None]) * masks[:, None, :]
    k_dec = np.exp((C - 1.0 - idx)[None, :, None] * log_gamma[:, None, None]) * scale * masks[:, None, :]
    chunk_decay = np.exp(C * log_gamma)
    return (jnp.asarray(decay, f32), jnp.asarray(q_dec, f32), jnp.asarray(k_dec, f32),
            [float(v) for v in chunk_decay])


def _retention_kernel(q_ref, k_ref, v_ref, g_ref, cos_ref, sin_ref, dec_ref, qd_ref, kd_ref, ng_ref,
                      o_ref, state_ref, *, chunk_decay):
    @pl.when(pl.program_id(1) == 0)
    def _():
        state_ref[...] = jnp.zeros_like(state_ref)

    cos = cos_ref[...]
    sin = sin_ref[...]
    lane = lax.broadcasted_iota(jnp.int32, (1, LANES), 1)
    own = ((lane % 64) < 32, (lane % 64) >= 32)
    for p in range(RET_HEADS // 2):
        sl = slice(p * LANES, (p + 1) * LANES)
        qp = q_ref[:, sl].astype(f32)
        kp = k_ref[:, sl].astype(f32)
        qr = qp * cos + pltpu.roll(qp, LANES // 2, 1) * sin
        kr = kp * cos + pltpu.roll(kp, LANES // 2, 1) * sin
        for hh in range(2):
            h = 2 * p + hh
            vs = slice(h * RET_V_DIM, (h + 1) * RET_V_DIM)
            qm = jnp.where(own[hh], qr, 0.0).astype(bf16)
            km = jnp.where(own[hh], kr, 0.0).astype(bf16)
            sc = lax.dot_general(qm, km, (((1,), (1,)), ((), ())), preferred_element_type=f32)
            sc = (sc * dec_ref[h]).astype(bf16)
            v = v_ref[:, vs]
            intra = jnp.dot(sc, v, preferred_element_type=f32)
            st = state_ref[h]
            qd = (qr * qd_ref[h]).astype(bf16)
            cross = jnp.dot(qd, st.astype(bf16), preferred_element_type=f32)
            y = intra + cross
            kd = (kr * kd_ref[h]).astype(bf16)
            state_ref[h] = chunk_decay[h] * st + lax.dot_general(
                kd, v, (((0,), (0,)), ((), ())), preferred_element_type=f32)
            ms = jnp.mean(y * y, axis=-1, keepdims=True)
            yn = y * lax.rsqrt(ms + RMS_EPS) * ng_ref[:, vs]
            g = g_ref[:, vs].astype(f32)
            o_ref[:, vs] = (g * jax.nn.sigmoid(g) * yn).astype(o_ref.dtype)


def retention(proj, cos_t, sin_t, ng_all, layer, *, B, S, C):
    nc = S // C
    dec, q_dec, k_dec, chunk_decay = _retention_tables(C)
    row = lambda b, c: b * nc + c
    const3 = lambda b, c: (0, 0, 0)
    return pl.pallas_call(
        functools.partial(_retention_kernel, chunk_decay=chunk_decay),
        grid=(B, nc),
        in_specs=[pl.BlockSpec((C, RET_QK_WIDTH), lambda b, c: (row(b, c), COL_QB // RET_QK_WIDTH)),
                  pl.BlockSpec((C, RET_QK_WIDTH), lambda b, c: (row(b, c), COL_KB // RET_QK_WIDTH)),
                  pl.BlockSpec((C, RET_V_WIDTH), lambda b, c: (row(b, c), COL_VB // RET_V_WIDTH)),
                  pl.BlockSpec((C, RET_V_WIDTH), lambda b, c: (row(b, c), COL_GB // RET_V_WIDTH)),
                  pl.BlockSpec((C, LANES), lambda b, c: (row(b, c), 0)),
                  pl.BlockSpec((C, LANES), lambda b, c: (row(b, c), 0)),
                  pl.BlockSpec((RET_HEADS, C, C), const3),
                  pl.BlockSpec((RET_HEADS, C, LANES), const3),
                  pl.BlockSpec((RET_HEADS, C, LANES), const3),
                  pl.BlockSpec((None, 1, RET_V_WIDTH), lambda b, c: (layer, 0, 0))],
        out_specs=pl.BlockSpec((C, RET_V_WIDTH), lambda b, c: (row(b, c), 0)),
        out_shape=jax.ShapeDtypeStruct((B * S, RET_V_WIDTH), bf16),
        scratch_shapes=[pltpu.VMEM((RET_HEADS, LANES, RET_V_DIM), f32)],
        compiler_params=_cparams(("parallel", "arbitrary")),
        name="retention",
    )(proj, proj, proj, proj, cos_t, sin_t, dec, q_dec, k_dec, ng_all)


def _layer_norm(y, g, b):
    mu = jnp.mean(y, axis=-1, keepdims=True)
    d = y - mu
    var = jnp.mean(d * d, axis=-1, keepdims=True)
    return d * lax.rsqrt(var + LN_EPS) * g + b


def _top2(logits):
    lane = lax.broadcasted_iota(jnp.int32, logits.shape, 1).astype(f32)
    big = float(LANES)
    lg = jnp.where(lane < N_EXPERTS, logits, -jnp.inf)
    m1 = jnp.max(lg, axis=-1, keepdims=True)
    e1 = jnp.min(jnp.where(lg == m1, lane, big), axis=-1, keepdims=True)
    lg2 = jnp.where(lane == e1, -jnp.inf, lg)
    m2 = jnp.max(lg2, axis=-1, keepdims=True)
    e2 = jnp.min(jnp.where(lg2 == m2, lane, big), axis=-1, keepdims=True)
    ex = jnp.exp(m2 - m1)
    den = 1.0 + ex
    return e1, e2, 1.0 / den, ex / den


def _merge_kernel(*refs, route):
    if route:
        (oa_ref, ob_ref, g0_ref, g1_ref, x_ref, wf_ref, wr_ref, wo_ref, lg_ref, lb_ref, wrt_ref,
         xo_ref, xb_ref, idx_ref, wts_ref) = refs
    else:
        (oa_ref, ob_ref, g0_ref, g1_ref, x_ref, wf_ref, wr_ref, wo_ref, lg_ref, lb_ref,
         xo_ref, xb_ref) = refs
    a = jnp.dot(oa_ref[...], wf_ref[...], preferred_element_type=f32)
    b = jnp.dot(ob_ref[...], wr_ref[...], preferred_element_type=f32)
    s0 = jax.nn.sigmoid(g0_ref[...].astype(f32))
    s1 = jax.nn.sigmoid(g1_ref[...].astype(f32))
    merged = (s0 * a + s1 * b).astype(bf16)
    h = jnp.dot(merged, wo_ref[...], preferred_element_type=f32)
    xn = _layer_norm(DN_ALPHA * x_ref[...] + h, lg_ref[...], lb_ref[...])
    xo_ref[...] = xn
    xnb = xn.astype(bf16)
    xb_ref[...] = xnb
    if route:
        logits = jnp.dot(xnb, wrt_ref[...], preferred_element_type=f32)
        e1, e2, w1, w2 = _top2(logits)
        lane = lax.broadcasted_iota(jnp.int32, logits.shape, 1)
        idx_ref[...] = jnp.where(lane == 0, e1, jnp.where(lane == 1, e2, 0.0)).astype(jnp.int32)
        wts_ref[...] = jnp.where(lane == 0, w1, jnp.where(lane == 1, w2, 0.0))


def merge_out(o_a, o_b, proj, x, wfox_all, wret_all, wout_all, lng_all, lnb_all, layer, *, tm,
              router_all=None, router_idx=0):
    M = x.shape[0]
    route = router_all is not None
    lw = lambda i: (layer, 0, 0)
    in_specs = [pl.BlockSpec((tm, FOX_WIDTH), lambda i: (i, 0)),
                pl.BlockSpec((tm, RET_V_WIDTH), lambda i: (i, 0)),
                pl.BlockSpec((tm, D_MODEL), lambda i: (i, COL_G0 // D_MODEL)),
                pl.BlockSpec((tm, D_MODEL), lambda i: (i, COL_G1 // D_MODEL)),
                pl.BlockSpec((tm, D_MODEL), lambda i: (i, 0)),
                pl.BlockSpec((None, FOX_WIDTH, D_MODEL), lw),
                pl.BlockSpec((None, RET_V_WIDTH, D_MODEL), lw),
                pl.BlockSpec((None, D_MODEL, D_MODEL), lw),
                pl.BlockSpec((None, 1, D_MODEL), lw),
                pl.BlockSpec((None, 1, D_MODEL), lw)]
    args = [o_a, o_b, proj, proj, x, wfox_all, wret_all, wout_all, lng_all, lnb_all]
    out_specs = [pl.BlockSpec((tm, D_MODEL), lambda i: (i, 0)), pl.BlockSpec((tm, D_MODEL), lambda i: (i, 0))]
    out_shape = [jax.ShapeDtypeStruct((M, D_MODEL), f32), jax.ShapeDtypeStruct((M, D_MODEL), bf16)]
    if route:
        in_specs.append(pl.BlockSpec((None, D_MODEL, LANES), lambda i: (router_idx, 0, 0)))
        args.append(router_all)
        out_specs += [pl.BlockSpec((tm, LANES), lambda i: (i, 0))] * 2
        out_shape += [jax.ShapeDtypeStruct((M, LANES), jnp.int32), jax.ShapeDtypeStruct((M, LANES), f32)]
    return pl.pallas_call(
        functools.partial(_merge_kernel, route=route),
        grid=(M // tm,),
        in_specs=in_specs,
        out_specs=out_specs,
        out_shape=out_shape,
        compiler_params=_cparams(("parallel",)),
        name="merge_out",
    )(*args)


def _swiglu_partial(x, wg, wu, wd):
    g = jnp.dot(x, wg, preferred_element_type=f32)
    u = jnp.dot(x, wu, preferred_element_type=f32)
    h = (g * jax.nn.sigmoid(g) * u).astype(bf16)
    return jnp.dot(h, wd, preferred_element_type=f32)


def _dense_ffn_kernel(xb_ref, x_ref, wg_ref, wu_ref, wd_ref, lg_ref, lb_ref, xo_ref, xbo_ref, acc_ref):
    f = pl.program_id(1)
    part = _swiglu_partial(xb_ref[...], wg_ref[...], wu_ref[...], wd_ref[...])

    @pl.when(f == 0)
    def _():
        acc_ref[...] = part

    @pl.when(f > 0)
    def _():
        acc_ref[...] += part

    @pl.when(f == pl.num_programs(1) - 1)
    def _():
        xn = _layer_norm(DN_ALPHA * x_ref[...] + acc_ref[...], lg_ref[...], lb_ref[...])
        xo_ref[...] = xn
        xbo_ref[...] = xn.astype(bf16)


def dense_ffn(xb, x, wg_all, wu_all, wd_all, lng_all, lnb_all, layer, widx, *, tm, tf):
    M = x.shape[0]
    F = wg_all.shape[2]
    return pl.pallas_call(
        _dense_ffn_kernel,
        grid=(M // tm, F // tf),
        in_specs=[pl.BlockSpec((tm, D_MODEL), lambda i, f: (i, 0)),
                  pl.BlockSpec((tm, D_MODEL), lambda i, f: (i, 0)),
                  pl.BlockSpec((None, D_MODEL, tf), lambda i, f: (widx, 0, f)),
                  pl.BlockSpec((None, D_MODEL, tf), lambda i, f: (widx, 0, f)),
                  pl.BlockSpec((None, tf, D_MODEL), lambda i, f: (widx, f, 0)),
                  pl.BlockSpec((None, 1, D_MODEL), lambda i, f: (layer, 0, 0)),
                  pl.BlockSpec((None, 1, D_MODEL), lambda i, f: (layer, 0, 0))],
        out_specs=[pl.BlockSpec((tm, D_MODEL), lambda i, f: (i, 0)),
                   pl.BlockSpec((tm, D_MODEL), lambda i, f: (i, 0))],
        out_shape=[jax.ShapeDtypeStruct((M, D_MODEL), f32), jax.ShapeDtypeStruct((M, D_MODEL), bf16)],
        scratch_shapes=[pltpu.VMEM((tm, D_MODEL), f32)],
        compiler_params=_cparams(("parallel", "arbitrary")),
        name="dense_ffn",
    )(xb, x, wg_all, wu_all, wd_all, lng_all, lnb_all)


def _row_slice(r):
    return pl.ds(pl.multiple_of(r * SUBLANES, SUBLANES), SUBLANES)


def _rows_to_tiles(dst_ref, val):
    tm = val.shape[0]
    for c in range(SUBLANES):
        dst_ref[pl.ds(c, tm, stride=SUBLANES), :] = val[:, c * LANES:(c + 1) * LANES]


def _tiles_to_rows(src_ref):
    tm = src_ref.shape[0] // SUBLANES
    return jnp.concatenate([src_ref[pl.ds(c, tm, stride=SUBLANES), :] for c in range(SUBLANES)], axis=-1)


def _dispatch_kernel(x_ref, pos_hbm, xs_in_hbm, xs_hbm, buf_ref, pos_smem, sem_pos, sem_rows, *, tm):
    del xs_in_hbm
    i = pl.program_id(0)
    cp = pltpu.make_async_copy(pos_hbm.at[i], pos_smem, sem_pos)
    cp.start()
    _rows_to_tiles(buf_ref, x_ref[...])
    cp.wait()

    def row_copy(t, k):
        return pltpu.make_async_copy(buf_ref.at[_row_slice(t), :],
                                     xs_hbm.at[_row_slice(pos_smem[TOP_K * t + k]), :], sem_rows)

    def issue(t, carry):
        for k in range(TOP_K):
            row_copy(t, k).start(priority=k % 2)
        return carry

    lax.fori_loop(0, tm, issue, 0)

    def drain(t, carry):
        for k in range(TOP_K):
            row_copy(t, k).wait()
        return carry

    lax.fori_loop(0, tm, drain, 0)


def moe_dispatch(x, pos_tiles, n_rows, *, tm):
    M = x.shape[0]
    zeros = jnp.zeros((n_rows * SUBLANES, LANES), f32)
    return pl.pallas_call(
        functools.partial(_dispatch_kernel, tm=tm),
        grid=(M // tm,),
        in_specs=[pl.BlockSpec((tm, D_MODEL), lambda i: (i, 0)),
                  pl.BlockSpec(memory_space=pl.ANY),
                  pl.BlockSpec(memory_space=pl.ANY)],
        out_specs=pl.BlockSpec(memory_space=pl.ANY),
        out_shape=jax.ShapeDtypeStruct((n_rows * SUBLANES, LANES), f32),
        scratch_shapes=[pltpu.VMEM((tm * SUBLANES, LANES), f32),
                        pltpu.SMEM((TOP_K * tm,), jnp.int32),
                        pltpu.SemaphoreType.DMA(()),
                        pltpu.SemaphoreType.DMA(())],
        input_output_aliases={2: 0},
        compiler_params=_cparams(("arbitrary",)),
        name="moe_dispatch",
    )(x, pos_tiles, zeros)


def _moe_ffn_kernel(te_ref, xs_ref, wg_ref, wu_ref, wd_ref, ys_ref, xb_ref, acc_ref, *, n_tiles):
    i = pl.program_id(0)
    f = pl.program_id(1)
    valid = te_ref[n_tiles + i] == 1

    @pl.when(valid)
    def _():
        @pl.when(f == 0)
        def _():
            xb_ref[...] = _tiles_to_rows(xs_ref).astype(bf16)

        part = _swiglu_partial(xb_ref[...], wg_ref[...], wu_ref[...], wd_ref[...])

        @pl.when(f == 0)
        def _():
            acc_ref[...] = part

        @pl.when(f > 0)
        def _():
            acc_ref[...] += part

    @pl.when(f == pl.num_programs(1) - 1)
    def _():
        @pl.when(valid)
        def _():
            _rows_to_tiles(ys_ref, acc_ref[...])

        @pl.when(jnp.logical_not(valid))
        def _():
            ys_ref[...] = jnp.zeros_like(ys_ref)


def moe_ffn(tile_info, xs, wg_all, wu_all, wd_all, widx, *, tm, tf):
    R = xs.shape[0] // SUBLANES
    n_tiles = R // tm
    F = wg_all.shape[3]
    grid_spec = pltpu.PrefetchScalarGridSpec(
        num_scalar_prefetch=1,
        grid=(n_tiles, F // tf),
        in_specs=[pl.BlockSpec((tm * SUBLANES, LANES), lambda i, f, te: (i, 0)),
                  pl.BlockSpec((None, None, D_MODEL, tf), lambda i, f, te: (widx, te[i], 0, f)),
                  pl.BlockSpec((None, None, D_MODEL, tf), lambda i, f, te: (widx, te[i], 0, f)),
                  pl.BlockSpec((None, None, tf, D_MODEL), lambda i, f, te: (widx, te[i], f, 0))],
        out_specs=pl.BlockSpec((tm * SUBLANES, LANES), lambda i, f, te: (i, 0)),
        scratch_shapes=[pltpu.VMEM((tm, D_MODEL), bf16), pltpu.VMEM((tm, D_MODEL), f32)],
    )
    return pl.pallas_call(
        functools.partial(_moe_ffn_kernel, n_tiles=n_tiles),
        grid_spec=grid_spec,
        out_shape=jax.ShapeDtypeStruct((R * SUBLANES, LANES), f32),
        compiler_params=_cparams(("arbitrary", "arbitrary")),
        name="moe_ffn",
    )(tile_info, xs, wg_all, wu_all, wd_all)


def _combine_kernel(x_ref, wts_ref, lg_ref, lb_ref, pos_hbm, ys_hbm, xo_ref, xbo_ref,
                    buf_ref, pos_smem, sem_pos, sem_rows, *, tm):
    i = pl.program_id(0)
    cp = pltpu.make_async_copy(pos_hbm.at[i], pos_smem, sem_pos)
    cp.start()
    cp.wait()

    def row_copy(t, k):
        return pltpu.make_async_copy(ys_hbm.at[_row_slice(pos_smem[TOP_K * t + k]), :],
                                     buf_ref.at[k, _row_slice(t), :], sem_rows)

    def issue(t, carry):
        for k in range(TOP_K):
            row_copy(t, k).start(priority=k % 2)
        return carry

    lax.fori_loop(0, tm, issue, 0)

    def drain(t, carry):
        for k in range(TOP_K):
            row_copy(t, k).wait()
        return carry

    lax.fori_loop(0, tm, drain, 0)

    wts = wts_ref[...]
    h = wts[:, 0:1] * _tiles_to_rows(buf_ref.at[0]) + wts[:, 1:2] * _tiles_to_rows(buf_ref.at[1])
    xn = _layer_norm(DN_ALPHA * x_ref[...] + h, lg_ref[...], lb_ref[...])
    xo_ref[...] = xn
    xbo_ref[...] = xn.astype(bf16)


def moe_combine(x, wts, ys, pos_tiles, lng_all, lnb_all, layer, *, tm):
    M = x.shape[0]
    return pl.pallas_call(
        functools.partial(_combine_kernel, tm=tm),
        grid=(M // tm,),
        in_specs=[pl.BlockSpec((tm, D_MODEL), lambda i: (i, 0)),
                  pl.BlockSpec((tm, LANES), lambda i: (i, 0)),
                  pl.BlockSpec((None, 1, D_MODEL), lambda i: (layer, 0, 0)),
                  pl.BlockSpec((None, 1, D_MODEL), lambda i: (layer, 0, 0)),
                  pl.BlockSpec(memory_space=pl.ANY),
                  pl.BlockSpec(memory_space=pl.ANY)],
        out_specs=[pl.BlockSpec((tm, D_MODEL), lambda i: (i, 0)),
                   pl.BlockSpec((tm, D_MODEL), lambda i: (i, 0))],
        out_shape=[jax.ShapeDtypeStruct((M, D_MODEL), f32), jax.ShapeDtypeStruct((M, D_MODEL), bf16)],
        scratch_shapes=[pltpu.VMEM((TOP_K, tm * SUBLANES, LANES), f32),
                        pltpu.SMEM((TOP_K * tm,), jnp.int32),
                        pltpu.SemaphoreType.DMA(()),
                        pltpu.SemaphoreType.DMA(())],
        compiler_params=_cparams(("arbitrary",)),
        name="moe_combine",
    )(x, wts, lng_all, lnb_all, pos_tiles, ys)


def _routing_plan(idx, *, tm_rows, n_tiles):
    e = idx[:, :TOP_K].reshape(-1)
    onehot = (e[:, None] == jnp.arange(N_EXPERTS, dtype=jnp.int32)[None, :]).astype(jnp.int32)
    csum = jnp.cumsum(onehot, axis=0)
    counts = csum[-1]
    rank = jnp.take_along_axis(csum - onehot, e[:, None], axis=1)[:, 0]
    tiles_e = (counts + tm_rows - 1) // tm_rows
    tile_end = jnp.cumsum(tiles_e)
    row_start = (tile_end - tiles_e) * tm_rows
    pos = row_start[e] + rank
    t = jnp.arange(n_tiles, dtype=jnp.int32)
    te = jnp.sum((t[:, None] >= tile_end[None, :]).astype(jnp.int32), axis=1)
    valid = (t < tile_end[-1]).astype(jnp.int32)
    last_e = jnp.max(jnp.where(tiles_e > 0, jnp.arange(N_EXPERTS, dtype=jnp.int32), 0))
    te = jnp.where(valid == 1, jnp.minimum(te, N_EXPERTS - 1), last_e)
    return pos.astype(jnp.int32), jnp.concatenate([te, valid]).astype(jnp.int32)


def _config(B, S):
    M = B * S
    tm_moe = min(1024, M)
    return dict(
        proj_tm=min(4096, M), proj_tn=512,
        fc_ts=min(512, S),
        fox_tq=min(512, S),
        ret_c=min(256, S),
        rope_tm=min(2048, M),
        merge_tm=min(1024, M),
        ffn_tm=min(512, M), ffn_tf=1408,
        moe_tm=tm_moe, moe_tf=896,
        moe_tiles=(M * TOP_K) // tm_moe + N_EXPERTS,
        disp_tm=min(512, M), comb_tm=min(256, M),
    )


def _permute_rotary_cols(w):
    lead = w.shape[:-1]
    n = len(lead)
    w = w.reshape(lead + (RET_HEADS // 2, 2, 2, RET_QK_DIM // 2))
    w = jnp.swapaxes(w, n + 1, n + 2)
    return w.reshape(lead + (RET_QK_WIDTH,))


def _prep_in_weights(w_in, b_forget):
    w_main = jnp.concatenate([
        w_in[:, :, _O_QA:_O_F],
        _permute_rotary_cols(w_in[:, :, _O_QB:_O_KB]),
        w_in[:, :, _O_VB:],
        _permute_rotary_cols(w_in[:, :, _O_KB:_O_VB]),
    ], axis=-1).astype(bf16)
    lane_head = np.full((LANES,), -1, np.int64)
    for h in range(FOX_HEADS):
        lane_head[_c_lane(h):_c_lane(h) + C_TERMS] = h
    used = jnp.asarray(lane_head >= 0)
    src = jnp.asarray(np.maximum(lane_head, 0))
    w_f = jnp.where(used, jnp.take(w_in[:, :, _O_F:_O_QB], src, axis=2), 0.0).astype(bf16)
    b_f = jnp.where(used, jnp.take(b_forget, src, axis=1), 0.0)[:, None, :]
    return w_main, w_f, b_f


def kernel(x, positions, w_in, b_forget, ret_norm_g, w_branch_fox, w_branch_ret, w_out, ln_mix_g, ln_mix_b,
           ffn_w_gate, ffn_w_up, ffn_w_down, moe_router, moe_w_gate, moe_w_up, moe_w_down, ln_ffn_g, ln_ffn_b):
    B, S, _ = x.shape
    M = B * S
    cfg = _config(B, S)

    w_main, w_f, b_f = _prep_in_weights(w_in, b_forget)
    ng = ret_norm_g[:, None, :]
    wfox = w_branch_fox.astype(bf16)
    wret = w_branch_ret.astype(bf16)
    wout = w_out.astype(bf16)
    lmg, lmb = ln_mix_g[:, None, :], ln_mix_b[:, None, :]
    lfg, lfb = ln_ffn_g[:, None, :], ln_ffn_b[:, None, :]
    fwg, fwu, fwd = ffn_w_gate.astype(bf16), ffn_w_up.astype(bf16), ffn_w_down.astype(bf16)
    mwg, mwu, mwd = moe_w_gate.astype(bf16), moe_w_up.astype(bf16), moe_w_down.astype(bf16)
    wrt = jnp.pad(moe_router, ((0, 0), (0, 0), (0, LANES - N_EXPERTS))).astype(bf16)

    cos_t, sin_t = rope_tables(positions, tm=cfg["rope_tm"])

    xf = x.reshape(M, D_MODEL)
    xb = xf.astype(bf16)
    for layer in range(DEPTH):
        proj = in_projection(xb, w_main, layer, tm=cfg["proj_tm"], tn=cfg["proj_tn"])
        c = forget_cumsum(xb, w_f, b_f, layer, B=B, S=S, ts=cfg["fc_ts"])
        o_a = fox_attention(proj, c, B=B, S=S, tq=cfg["fox_tq"])
        o_b = retention(proj, cos_t, sin_t, ng, layer, B=B, S=S, C=cfg["ret_c"])
        i = layer // 2
        if layer % 2 == 0:
            xf, xb = merge_out(o_a, o_b, proj, xf, wfox, wret, wout, lmg, lmb, layer, tm=cfg["merge_tm"])
            xf, xb = dense_ffn(xb, xf, fwg, fwu, fwd, lfg, lfb, layer, i, tm=cfg["ffn_tm"], tf=cfg["ffn_tf"])
        else:
            xf, _, idx, wts = merge_out(o_a, o_b, proj, xf, wfox, wret, wout, lmg, lmb, layer,
                                        tm=cfg["merge_tm"], router_all=wrt, router_idx=i)
            n_tiles = cfg["moe_tiles"]
            pos, tile_info = _routing_plan(idx, tm_rows=cfg["moe_tm"], n_tiles=n_tiles)
            xs = moe_dispatch(xf, pos.reshape(M // cfg["disp_tm"], TOP_K * cfg["disp_tm"]),
                              n_tiles * cfg["moe_tm"], tm=cfg["disp_tm"])
            ys = moe_ffn(tile_info, xs, mwg, mwu, mwd, i, tm=cfg["moe_tm"], tf=cfg["moe_tf"])
            xf, xb = moe_combine(xf, wts, ys, pos.reshape(M // cfg["comb_tm"], TOP_K * cfg["comb_tm"]),
                                 lfg, lfb, layer, tm=cfg["comb_tm"])
    return xf.reshape(B, S, D_MODEL)
```

```python
import functools
import math

import numpy as np
import jax
import jax.numpy as jnp
from jax import lax
from jax.experimental import pallas as pl
from jax.experimental.pallas import tpu as pltpu

f32 = jnp.float32
bf16 = jnp.bfloat16

D_MODEL = 1024
DEPTH = 4
FOX_HEADS = 8
FOX_HEAD_DIM = 64
FOX_WIDTH = FOX_HEADS * FOX_HEAD_DIM
RET_HEADS = 8
RET_QK_DIM = 64
RET_V_DIM = 128
RET_QK_WIDTH = RET_HEADS * RET_QK_DIM
RET_V_WIDTH = RET_HEADS * RET_V_DIM
ROPE_BASE = 10000.0
N_BRANCH = 2
D_FF = 2816
N_EXPERTS = 8
TOP_K = 2
D_FF_EXPERT = 3584
DN_ALPHA = (2 * DEPTH) ** 0.25
LN_EPS = 1e-5
RMS_EPS = 1e-6

LANES = 128
SUBLANES = 8
VMEM_LIMIT = 56 * 1024 * 1024

COL_QA = 0
COL_KA = 512
COL_VA = 1024
COL_QB = 1536
COL_VB = 2048
COL_GB = 3072
COL_G0 = 4096
COL_G1 = 5120
COL_KB = 6144
PROJ_COLS = 6656

_O_QA, _O_KA, _O_VA, _O_F, _O_QB, _O_KB, _O_VB, _O_GB, _O_GATE = 0, 512, 1024, 1536, 1544, 2056, 2568, 3592, 4616


def _cparams(sem, vmem=VMEM_LIMIT):
    return pltpu.CompilerParams(dimension_semantics=sem, vmem_limit_bytes=vmem)


def _mm_kernel(x_ref, w_ref, o_ref):
    o_ref[...] = jnp.dot(x_ref[...], w_ref[...], preferred_element_type=f32).astype(o_ref.dtype)


def in_projection(xb, w_all, layer, *, tm, tn):
    M, K = xb.shape
    N = w_all.shape[2]
    return pl.pallas_call(
        _mm_kernel,
        grid=(M // tm, N // tn),
        in_specs=[pl.BlockSpec((tm, K), lambda i, j: (i, 0)),
                  pl.BlockSpec((None, K, tn), lambda i, j: (layer, 0, j))],
        out_specs=pl.BlockSpec((tm, tn), lambda i, j: (i, j)),
        out_shape=jax.ShapeDtypeStruct((M, N), bf16),
        compiler_params=_cparams(("parallel", "arbitrary")),
        name="in_projection",
    )(xb, w_all)


C_TERMS = 3
LOG2E = math.log2(math.e)


def _c_lane(h):
    return (FOX_HEAD_DIM if h % 2 == 0 else 0) + C_TERMS * (h // 2)


def _split_bf16(v):
    hi = v.astype(bf16)
    r1 = v - hi.astype(f32)
    mid = r1.astype(bf16)
    lo = (r1 - mid.astype(f32)).astype(bf16)
    return hi, mid, lo


def _forget_cumsum_kernel(x_ref, wf_ref, bf_ref, term_ref, o_ref, carry_ref, *, ts):
    @pl.when(pl.program_id(1) == 0)
    def _():
        carry_ref[...] = jnp.zeros_like(carry_ref)

    term = term_ref[...]
    z = jnp.dot(x_ref[...], wf_ref[...], preferred_element_type=f32) + bf_ref[...]
    lf = jnp.minimum(z, 0.0) - jnp.log1p(jnp.exp(-jnp.abs(z)))
    lf = jnp.where(term >= 0, lf, 0.0)
    row = lax.broadcasted_iota(jnp.int32, (ts, ts), 0)
    col = lax.broadcasted_iota(jnp.int32, (ts, ts), 1)
    tri = jnp.where(col <= row, 1.0, 0.0).astype(bf16)
    c = carry_ref[...]
    for part in _split_bf16(lf):
        c = c + jnp.dot(tri, part, preferred_element_type=f32)
    carry_ref[...] = c[ts - 1:ts, :]
    hi, mid, lo = _split_bf16(c * LOG2E)
    o_ref[...] = jnp.where(term == 0, hi, jnp.where(term == 1, mid, jnp.where(term == 2, lo, jnp.zeros_like(hi))))


def forget_cumsum(xb, wf_all, bf_all, layer, *, B, S, ts):
    K = xb.shape[1]
    nt = S // ts
    term = np.full((1, LANES), -1, np.int32)
    for h in range(FOX_HEADS):
        term[0, _c_lane(h):_c_lane(h) + C_TERMS] = np.arange(C_TERMS)
    return pl.pallas_call(
        functools.partial(_forget_cumsum_kernel, ts=ts),
        grid=(B, nt),
        in_specs=[pl.BlockSpec((ts, K), lambda b, j: (b * nt + j, 0)),
                  pl.BlockSpec((None, K, LANES), lambda b, j: (layer, 0, 0)),
                  pl.BlockSpec((None, 1, LANES), lambda b, j: (layer, 0, 0)),
                  pl.BlockSpec((1, LANES), lambda b, j: (0, 0))],
        out_specs=pl.BlockSpec((ts, LANES), lambda b, j: (b * nt + j, 0)),
        out_shape=jax.ShapeDtypeStruct((B * S, LANES), bf16),
        scratch_shapes=[pltpu.VMEM((1, LANES), f32)],
        compiler_params=_cparams(("parallel", "arbitrary")),
        name="forget_cumsum",
    )(xb, wf_all, bf_all, jnp.asarray(term))


def _fox_kernel(q_ref, k_ref, v_ref, ca_ref, o_ref, m_ref, acc_ref, *, tq):
    pair = pl.program_id(1)
    i = pl.program_id(2)
    lane = lax.broadcasted_iota(jnp.int32, (1, LANES), 1)
    head_lanes = (lane < FOX_HEAD_DIM, lane >= FOX_HEAD_DIM)
    q = q_ref[...] * jnp.asarray(FOX_HEAD_DIM ** -0.5 * LOG2E, bf16)
    qh = []
    for h in range(2):
        first = (FOX_HEAD_DIM if h == 0 else 0) + C_TERMS * pair
        minus_one = jnp.where((lane >= first) & (lane < first + C_TERMS), -1.0, 0.0).astype(bf16)
        qh.append(jnp.where(head_lanes[h], q, minus_one))

    m_ref[...] = jnp.full_like(m_ref, -jnp.inf)
    acc_ref[...] = jnp.zeros_like(acc_ref)

    key = lax.broadcasted_iota(jnp.int32, (tq, tq), 0)
    qry = lax.broadcasted_iota(jnp.int32, (tq, tq), 1)
    causal = key <= qry

    def chunks(js, masked):
        sts, vcs = [], []
        for j in js:
            k0 = pl.multiple_of(j * tq, tq)
            kc = k_ref[pl.ds(k0, tq), :]
            vcs.append(v_ref[pl.ds(k0, tq), :])
            ca = ca_ref[pl.ds(k0, tq), :]
            for h in range(2):
                kh = jnp.where(head_lanes[h], kc, ca)
                st = lax.dot_general(kh, qh[h], (((1,), (1,)), ((), ())), preferred_element_type=f32)
                if masked:
                    st = jnp.where(causal, st, -jnp.inf)
                sts.append(st)
        for n in range(len(js)):
            for h in range(2):
                st = sts[2 * n + h]
                m_prev = m_ref[h]
                m_new = jnp.maximum(m_prev, jnp.max(st, axis=0, keepdims=True))
                alpha = jnp.exp2(m_prev - m_new)
                p = jnp.exp2(st - m_new).astype(bf16)
                vh = jnp.where(head_lanes[h], vcs[n], jnp.ones_like(vcs[n]))
                pv = lax.dot_general(vh, p, (((0,), (0,)), ((), ())), preferred_element_type=f32)
                acc_ref[h] = alpha * acc_ref[h] + pv
                m_ref[h] = m_new

    chunks([i], True)

    @pl.when(i % 2 == 1)
    def _():
        chunks([i - 1], False)

    def body(jj, carry):
        chunks([2 * jj, 2 * jj + 1], False)
        return carry

    lax.fori_loop(0, i // 2, body, 0)

    a0, a1 = acc_ref[0], acc_ref[1]
    o0 = a0 / a0[FOX_HEAD_DIM:FOX_HEAD_DIM + 1, :]
    o1 = a1 / a1[0:1, :]
    sub = lax.broadcasted_iota(jnp.int32, (LANES, 1), 0)
    ot = jnp.where(sub < FOX_HEAD_DIM, o0, o1)
    o_ref[...] = ot.T.astype(o_ref.dtype)


def fox_attention(proj, caug, *, B, S, tq):
    nq = S // tq
    npair = FOX_HEADS // 2
    return pl.pallas_call(
        functools.partial(_fox_kernel, tq=tq),
        grid=(B, npair, nq),
        in_specs=[pl.BlockSpec((tq, LANES), lambda b, p, i: (b * nq + i, COL_QA // LANES + p)),
                  pl.BlockSpec((S, LANES), lambda b, p, i: (b, COL_KA // LANES + p)),
                  pl.BlockSpec((S, LANES), lambda b, p, i: (b, COL_VA // LANES + p)),
                  pl.BlockSpec((S, LANES), lambda b, p, i: (b, 0))],
        out_specs=pl.BlockSpec((tq, LANES), lambda b, p, i: (b * nq + i, p)),
        out_shape=jax.ShapeDtypeStruct((B * S, FOX_WIDTH), bf16),
        scratch_shapes=[pltpu.VMEM((2, 1, tq), f32), pltpu.VMEM((2, LANES, tq), f32)],
        compiler_params=_cparams(("parallel", "parallel", "arbitrary")),
        name="fox_attention",
    )(proj, proj, proj, caug)


def _rope_table_kernel(pos_ref, freq_ref, sign_ref, cos_ref, sin_ref):
    ang = pos_ref[...].astype(f32) * freq_ref[...]
    cos_ref[...] = jnp.cos(ang)
    sin_ref[...] = jnp.sin(ang) * sign_ref[...]


def rope_tables(positions, *, tm):
    n = positions.size
    half = RET_QK_DIM // 2
    inv_freq = ROPE_BASE ** (-jnp.arange(half, dtype=f32) / half)
    freq = jnp.tile(inv_freq, LANES // half).reshape(1, LANES)
    sign = jnp.asarray(np.where(np.arange(LANES) < LANES // 2, -1.0, 1.0), f32).reshape(1, LANES)
    pos = positions.reshape(n, 1)
    out = jax.ShapeDtypeStruct((n, LANES), f32)
    return pl.pallas_call(
        _rope_table_kernel,
        grid=(n // tm,),
        in_specs=[pl.BlockSpec((tm, 1), lambda i: (i, 0)),
                  pl.BlockSpec((1, LANES), lambda i: (0, 0)),
                  pl.BlockSpec((1, LANES), lambda i: (0, 0))],
        out_specs=[pl.BlockSpec((tm, LANES), lambda i: (i, 0))] * 2,
        out_shape=[out, out],
        compiler_params=_cparams(("parallel",)),
        name="rope_tables",
    )(pos, freq, sign)


def _retention_tables(C):
    h = np.arange(RET_HEADS, dtype=np.float64)
    log_gamma = np.log(1.0 - 2.0 ** (-5.0 - h))
    idx = np.arange(C, dtype=np.float64)
    diff = idx[:, None] - idx[None, :]
    scale = RET_QK_DIM ** -0.5
    decay = np.where(diff >= 0, np.exp(np.maximum(diff, 0.0)[None] * log_gamma[:, None, None]), 0.0) * scale
    lane = np.arange(LANES)
    head_lane = (lane % 64) // 32
    masks = np.stack([(head_lane == (hh % 2)) for hh in range(RET_HEADS)]).astype(np.float64)
    q_dec = np.exp((idx + 1.0)[None, :, None] * log_gamma[:, None, None]) * masks[:, None, :]
    k_dec = np.exp((C - 1.0 - idx)[None, :, None] * log_gamma[:, None, None]) * scale * masks[:, None, :]
    chunk_decay = np.exp(C * log_gamma)
    return (jnp.asarray(decay, f32), jnp.asarray(q_dec, f32), jnp.asarray(k_dec, f32),
            [float(v) for v in chunk_decay])


def _retention_kernel(q_ref, k_ref, v_ref, g_ref, cos_ref, sin_ref, dec_ref, qd_ref, kd_ref, ng_ref,
                      o_ref, state_ref, *, chunk_decay):
    @pl.when(pl.program_id(1) == 0)
    def _():
        state_ref[...] = jnp.zeros_like(state_ref)

    cos = cos_ref[...]
    sin = sin_ref[...]
    lane = lax.broadcasted_iota(jnp.int32, (1, LANES), 1)
    own = ((lane % 64) < 32, (lane % 64) >= 32)
    for p in range(RET_HEADS // 2):
        sl = slice(p * LANES, (p + 1) * LANES)
        qp = q_ref[:, sl].astype(f32)
        kp = k_ref[:, sl].astype(f32)
        qr = qp * cos + pltpu.roll(qp, LANES // 2, 1) * sin
        kr = kp * cos + pltpu.roll(kp, LANES // 2, 1) * sin
        for hh in range(2):
            h = 2 * p + hh
            vs = slice(h * RET_V_DIM, (h + 1) * RET_V_DIM)
            qm = jnp.where(own[hh], qr, 0.0).astype(bf16)
            km = jnp.where(own[hh], kr, 0.0).astype(bf16)
            sc = lax.dot_general(qm, km, (((1,), (1,)), ((), ())), preferred_element_type=f32)
            sc = (sc * dec_ref[h]).astype(bf16)
            v = v_ref[:, vs]
            intra = jnp.dot(sc, v, preferred_element_type=f32)
            st = state_ref[h]
            qd = (qr * qd_ref[h]).astype(bf16)
            cross = jnp.dot(qd, st.astype(bf16), preferred_element_type=f32)
            y = intra + cross
            kd = (kr * kd_ref[h]).astype(bf16)
            state_ref[h] = chunk_decay[h] * st + lax.dot_general(
                kd, v, (((0,), (0,)), ((), ())), preferred_element_type=f32)
            ms = jnp.mean(y * y, axis=-1, keepdims=True)
            yn = y * lax.rsqrt(ms + RMS_EPS) * ng_ref[:, vs]
            g = g_ref[:, vs].astype(f32)
            o_ref[:, vs] = (g * jax.nn.sigmoid(g) * yn).astype(o_ref.dtype)


def retention(proj, cos_t, sin_t, ng_all, layer, *, B, S, C):
    nc = S // C
    dec, q_dec, k_dec, chunk_decay = _retention_tables(C)
    row = lambda b, c: b * nc + c
    const3 = lambda b, c: (0, 0, 0)
    return pl.pallas_call(
        functools.partial(_retention_kernel, chunk_decay=chunk_decay),
        grid=(B, nc),
        in_specs=[pl.BlockSpec((C, RET_QK_WIDTH), lambda b, c: (row(b, c), COL_QB // RET_QK_WIDTH)),
                  pl.BlockSpec((C, RET_QK_WIDTH), lambda b, c: (row(b, c), COL_KB // RET_QK_WIDTH)),
                  pl.BlockSpec((C, RET_V_WIDTH), lambda b, c: (row(b, c), COL_VB // RET_V_WIDTH)),
                  pl.BlockSpec((C, RET_V_WIDTH), lambda b, c: (row(b, c), COL_GB // RET_V_WIDTH)),
                  pl.BlockSpec((C, LANES), lambda b, c: (row(b, c), 0)),
                  pl.BlockSpec((C, LANES), lambda b, c: (row(b, c), 0)),
                  pl.BlockSpec((RET_HEADS, C, C), const3),
                  pl.BlockSpec((RET_HEADS, C, LANES), const3),
                  pl.BlockSpec((RET_HEADS, C, LANES), const3),
                  pl.BlockSpec((None, 1, RET_V_WIDTH), lambda b, c: (layer, 0, 0))],
        out_specs=pl.BlockSpec((C, RET_V_WIDTH), lambda b, c: (row(b, c), 0)),
        out_shape=jax.ShapeDtypeStruct((B * S, RET_V_WIDTH), bf16),
        scratch_shapes=[pltpu.VMEM((RET_HEADS, LANES, RET_V_DIM), f32)],
        compiler_params=_cparams(("parallel", "arbitrary")),
        name="retention",
    )(proj, proj, proj, proj, cos_t, sin_t, dec, q_dec, k_dec, ng_all)


def _layer_norm(y, g, b):
    mu = jnp.mean(y, axis=-1, keepdims=True)
    d = y - mu
    var = jnp.mean(d * d, axis=-1, keepdims=True)
    return d * lax.rsqrt(var + LN_EPS) * g + b


def _top2(logits):
    lane = lax.broadcasted_iota(jnp.int32, logits.shape, 1).astype(f32)
    big = float(LANES)
    lg = jnp.where(lane < N_EXPERTS, logits, -jnp.inf)
    m1 = jnp.max(lg, axis=-1, keepdims=True)
    e1 = jnp.min(jnp.where(lg == m1, lane, big), axis=-1, keepdims=True)
    lg2 = jnp.where(lane == e1, -jnp.inf, lg)
    m2 = jnp.max(lg2, axis=-1, keepdims=True)
    e2 = jnp.min(jnp.where(lg2 == m2, lane, big), axis=-1, keepdims=True)
    ex = jnp.exp(m2 - m1)
    den = 1.0 + ex
    return e1, e2, 1.0 / den, ex / den


def _merge_kernel(*refs, route):
    if route:
        (oa_ref, ob_ref, g0_ref, g1_ref, x_ref, wf_ref, wr_ref, wo_ref, lg_ref, lb_ref, wrt_ref,
         xo_ref, xb_ref, idx_ref, wts_ref) = refs
    else:
        (oa_ref, ob_ref, g0_ref, g1_ref, x_ref, wf_ref, wr_ref, wo_ref, lg_ref, lb_ref,
         xo_ref, xb_ref) = refs
    a = jnp.dot(oa_ref[...], wf_ref[...], preferred_element_type=f32)
    b = jnp.dot(ob_ref[...], wr_ref[...], preferred_element_type=f32)
    s0 = jax.nn.sigmoid(g0_ref[...].astype(f32))
    s1 = jax.nn.sigmoid(g1_ref[...].astype(f32))
    merged = (s0 * a + s1 * b).astype(bf16)
    h = jnp.dot(merged, wo_ref[...], preferred_element_type=f32)
    xn = _layer_norm(DN_ALPHA * x_ref[...] + h, lg_ref[...], lb_ref[...])
    xo_ref[...] = xn
    xnb = xn.astype(bf16)
    xb_ref[...] = xnb
    if route:
        logits = jnp.dot(xnb, wrt_ref[...], preferred_element_type=f32)
        e1, e2, w1, w2 = _top2(logits)
        lane = lax.broadcasted_iota(jnp.int32, logits.shape, 1)
        idx_ref[...] = jnp.where(lane == 0, e1, jnp.where(lane == 1, e2, 0.0)).astype(jnp.int32)
        wts_ref[...] = jnp.where(lane == 0, w1, jnp.where(lane == 1, w2, 0.0))


def merge_out(o_a, o_b, proj, x, wfox_all, wret_all, wout_all, lng_all, lnb_all, layer, *, tm,
              router_all=None, router_idx=0):
    M = x.shape[0]
    route = router_all is not None
    lw = lambda i: (layer, 0, 0)
    in_specs = [pl.BlockSpec((tm, FOX_WIDTH), lambda i: (i, 0)),
                pl.BlockSpec((tm, RET_V_WIDTH), lambda i: (i, 0)),
                pl.BlockSpec((tm, D_MODEL), lambda i: (i, COL_G0 // D_MODEL)),
                pl.BlockSpec((tm, D_MODEL), lambda i: (i, COL_G1 // D_MODEL)),
                pl.BlockSpec((tm, D_MODEL), lambda i: (i, 0)),
                pl.BlockSpec((None, FOX_WIDTH, D_MODEL), lw),
                pl.BlockSpec((None, RET_V_WIDTH, D_MODEL), lw),
                pl.BlockSpec((None, D_MODEL, D_MODEL), lw),
                pl.BlockSpec((None, 1, D_MODEL), lw),
                pl.BlockSpec((None, 1, D_MODEL), lw)]
    args = [o_a, o_b, proj, proj, x, wfox_all, wret_all, wout_all, lng_all, lnb_all]
    out_specs = [pl.BlockSpec((tm, D_MODEL), lambda i: (i, 0)), pl.BlockSpec((tm, D_MODEL), lambda i: (i, 0))]
    out_shape = [jax.ShapeDtypeStruct((M, D_MODEL), f32), jax.ShapeDtypeStruct((M, D_MODEL), bf16)]
    if route:
        in_specs.append(pl.BlockSpec((None, D_MODEL, LANES), lambda i: (router_idx, 0, 0)))
        args.append(router_all)
        out_specs += [pl.BlockSpec((tm, LANES), lambda i: (i, 0))] * 2
        out_shape += [jax.ShapeDtypeStruct((M, LANES), jnp.int32), jax.ShapeDtypeStruct((M, LANES), f32)]
    return pl.pallas_call(
        functools.partial(_merge_kernel, route=route),
        grid=(M // tm,),
        in_specs=in_specs,
        out_specs=out_specs,
        out_shape=out_shape,
        compiler_params=_cparams(("parallel",)),
        name="merge_out",
    )(*args)


def _swiglu_partial(x, wg, wu, wd):
    g = jnp.dot(x, wg, preferred_element_type=f32)
    u = jnp.dot(x, wu, preferred_element_type=f32)
    h = (g * jax.nn.sigmoid(g) * u).astype(bf16)
    return jnp.dot(h, wd, preferred_element_type=f32)


def _dense_ffn_kernel(xb_ref, x_ref, wg_ref, wu_ref, wd_ref, lg_ref, lb_ref, xo_ref, xbo_ref, acc_ref):
    f = pl.program_id(1)
    @pl.when(f == 0)
    def _():
        acc_ref[...] = jnp.zeros_like(acc_ref)

    acc_ref[...] += _swiglu_partial(xb_ref[...], wg_ref[...], wu_ref[...], wd_ref[...])

    @pl.when(f == pl.num_programs(1) - 1)
    def _():
        xn = _layer_norm(DN_ALPHA * x_ref[...] + acc_ref[...], lg_ref[...], lb_ref[...])
        xo_ref[...] = xn
        xbo_ref[...] = xn.astype(bf16)


def dense_ffn(xb, x, wg_all, wu_all, wd_all, lng_all, lnb_all, layer, widx, *, tm, tf):
    M = x.shape[0]
    F = wg_all.shape[2]
    return pl.pallas_call(
        _dense_ffn_kernel,
        grid=(M // tm, F // tf),
        in_specs=[pl.BlockSpec((tm, D_MODEL), lambda i, f: (i, 0)),
                  pl.BlockSpec((tm, D_MODEL), lambda i, f: (i, 0)),
                  pl.BlockSpec((None, D_MODEL, tf), lambda i, f: (widx, 0, f)),
                  pl.BlockSpec((None, D_MODEL, tf), lambda i, f: (widx, 0, f)),
                  pl.BlockSpec((None, tf, D_MODEL), lambda i, f: (widx, f, 0)),
                  pl.BlockSpec((None, 1, D_MODEL), lambda i, f: (layer, 0, 0)),
                  pl.BlockSpec((None, 1, D_MODEL), lambda i, f: (layer, 0, 0))],
        out_specs=[pl.BlockSpec((tm, D_MODEL), lambda i, f: (i, 0)),
                   pl.BlockSpec((tm, D_MODEL), lambda i, f: (i, 0))],
        out_shape=[jax.ShapeDtypeStruct((M, D_MODEL), f32), jax.ShapeDtypeStruct((M, D_MODEL), bf16)],
        scratch_shapes=[pltpu.VMEM((tm, D_MODEL), f32)],
        compiler_params=_cparams(("parallel", "arbitrary")),
        name="dense_ffn",
    )(xb, x, wg_all, wu_all, wd_all, lng_all, lnb_all)


DMA_UNROLL = 8


def _row_slice(r):
    return pl.ds(pl.multiple_of(r * SUBLANES, SUBLANES), SUBLANES)


def _rows_to_tiles(dst_ref, val):
    tm = val.shape[0]
    for c in range(SUBLANES):
        dst_ref[pl.ds(c, tm, stride=SUBLANES), :] = val[:, c * LANES:(c + 1) * LANES]


def _tiles_to_rows(src_ref):
    tm = src_ref.shape[0] // SUBLANES
    return jnp.concatenate([src_ref[pl.ds(c, tm, stride=SUBLANES), :] for c in range(SUBLANES)], axis=-1)


def _dispatch_kernel(pos_ref, x_ref, xs_in_hbm, xs_hbm, buf_ref, sem, *, tm):
    del xs_in_hbm
    i = pl.program_id(0)
    n = pl.num_programs(0)

    def row_copy(step, slot, t, k):
        dst = pos_ref[(step * tm + t) * TOP_K + k]
        return pltpu.make_async_copy(buf_ref.at[slot, _row_slice(t), :], xs_hbm.at[_row_slice(dst), :], sem.at[slot])

    def drain(step, slot):
        def body(t, carry):
            for k in range(TOP_K):
                row_copy(step, slot, t, k).wait()
            return carry
        lax.fori_loop(0, tm, body, 0, unroll=DMA_UNROLL)

    slot = i % 2
    _rows_to_tiles(buf_ref.at[slot], x_ref[...])

    def issue(t, carry):
        for k in range(TOP_K):
            row_copy(i, slot, t, k).start(priority=k % 2)
        return carry

    lax.fori_loop(0, tm, issue, 0, unroll=DMA_UNROLL)

    @pl.when(i > 0)
    def _():
        drain(i - 1, 1 - slot)

    @pl.when(i == n - 1)
    def _():
        drain(i, slot)


def moe_dispatch(x, pos, n_rows, *, tm):
    M = x.shape[0]
    zeros = jnp.zeros((n_rows * SUBLANES, LANES), f32)
    grid_spec = pltpu.PrefetchScalarGridSpec(
        num_scalar_prefetch=1,
        grid=(M // tm,),
        in_specs=[pl.BlockSpec((tm, D_MODEL), lambda i, pos: (i, 0)),
                  pl.BlockSpec(memory_space=pl.ANY)],
        out_specs=pl.BlockSpec(memory_space=pl.ANY),
        scratch_shapes=[pltpu.VMEM((2, tm * SUBLANES, LANES), f32),
                        pltpu.SemaphoreType.DMA((2,))],
    )
    return pl.pallas_call(
        functools.partial(_dispatch_kernel, tm=tm),
        grid_spec=grid_spec,
        out_shape=jax.ShapeDtypeStruct((n_rows * SUBLANES, LANES), f32),
        input_output_aliases={2: 0},
        compiler_params=_cparams(("arbitrary",)),
        name="moe_dispatch",
    )(pos, x, zeros)


def _moe_ffn_kernel(te_ref, xs_ref, wg_ref, wu_ref, wd_ref, ys_ref, xb_ref, acc_ref, *, n_tiles):
    i = pl.program_id(0)
    f = pl.program_id(1)
    valid = te_ref[n_tiles + i] == 1

    @pl.when(valid)
    def _():
        @pl.when(f == 0)
        def _():
            xb_ref[...] = _tiles_to_rows(xs_ref).astype(bf16)
            acc_ref[...] = jnp.zeros_like(acc_ref)

        acc_ref[...] += _swiglu_partial(xb_ref[...], wg_ref[...], wu_ref[...], wd_ref[...])

    @pl.when(f == pl.num_programs(1) - 1)
    def _():
        @pl.when(valid)
        def _():
            _rows_to_tiles(ys_ref, acc_ref[...])

        @pl.when(jnp.logical_not(valid))
        def _():
            ys_ref[...] = jnp.zeros_like(ys_ref)


def moe_ffn(tile_info, xs, wg_all, wu_all, wd_all, widx, *, tm, tf):
    R = xs.shape[0] // SUBLANES
    n_tiles = R // tm
    F = wg_all.shape[3]
    grid_spec = pltpu.PrefetchScalarGridSpec(
        num_scalar_prefetch=1,
        grid=(n_tiles, F // tf),
        in_specs=[pl.BlockSpec((tm * SUBLANES, LANES), lambda i, f, te: (i, 0)),
                  pl.BlockSpec((None, None, D_MODEL, tf), lambda i, f, te: (widx, te[i], 0, f)),
                  pl.BlockSpec((None, None, D_MODEL, tf), lambda i, f, te: (widx, te[i], 0, f)),
                  pl.BlockSpec((None, None, tf, D_MODEL), lambda i, f, te: (widx, te[i], f, 0))],
        out_specs=pl.BlockSpec((tm * SUBLANES, LANES), lambda i, f, te: (i, 0)),
        scratch_shapes=[pltpu.VMEM((tm, D_MODEL), bf16), pltpu.VMEM((tm, D_MODEL), f32)],
    )
    return pl.pallas_call(
        functools.partial(_moe_ffn_kernel, n_tiles=n_tiles),
        grid_spec=grid_spec,
        out_shape=jax.ShapeDtypeStruct((R * SUBLANES, LANES), f32),
        compiler_params=_cparams(("arbitrary", "arbitrary")),
        name="moe_ffn",
    )(tile_info, xs, wg_all, wu_all, wd_all)


def _combine_kernel(pos_ref, x_ref, wts_ref, lg_ref, lb_ref, ys_hbm, xo_ref, xbo_ref, buf_ref, sem, *, tm):
    i = pl.program_id(0)
    n = pl.num_programs(0)

    def row_copy(step, slot, t, k):
        src = pos_ref[(step * tm + t) * TOP_K + k]
        return pltpu.make_async_copy(ys_hbm.at[_row_slice(src), :], buf_ref.at[slot, k, _row_slice(t), :], sem.at[slot])

    def issue(step, slot):
        def body(t, carry):
            for k in range(TOP_K):
                row_copy(step, slot, t, k).start(priority=k % 2)
            return carry
        lax.fori_loop(0, tm, body, 0, unroll=DMA_UNROLL)

    slot = i % 2

    @pl.when(i == 0)
    def _():
        issue(0, 0)

    @pl.when(i + 1 < n)
    def _():
        issue(i + 1, 1 - slot)

    def drain(t, carry):
        for k in range(TOP_K):
            row_copy(i, slot, t, k).wait()
        return carry

    lax.fori_loop(0, tm, drain, 0, unroll=DMA_UNROLL)

    wts = wts_ref[...]
    h = (wts[:, 0:1] * _tiles_to_rows(buf_ref.at[slot, 0]) + wts[:, 1:2] * _tiles_to_rows(buf_ref.at[slot, 1]))
    xn = _layer_norm(DN_ALPHA * x_ref[...] + h, lg_ref[...], lb_ref[...])
    xo_ref[...] = xn
    xbo_ref[...] = xn.astype(bf16)


def moe_combine(x, wts, ys, pos, lng_all, lnb_all, layer, *, tm):
    M = x.shape[0]
    grid_spec = pltpu.PrefetchScalarGridSpec(
        num_scalar_prefetch=1,
        grid=(M // tm,),
        in_specs=[pl.BlockSpec((tm, D_MODEL), lambda i, pos: (i, 0)),
                  pl.BlockSpec((tm, LANES), lambda i, pos: (i, 0)),
                  pl.BlockSpec((None, 1, D_MODEL), lambda i, pos: (layer, 0, 0)),
                  pl.BlockSpec((None, 1, D_MODEL), lambda i, pos: (layer, 0, 0)),
                  pl.BlockSpec(memory_space=pl.ANY)],
        out_specs=[pl.BlockSpec((tm, D_MODEL), lambda i, pos: (i, 0)),
                   pl.BlockSpec((tm, D_MODEL), lambda i, pos: (i, 0))],
        scratch_shapes=[pltpu.VMEM((2, TOP_K, tm * SUBLANES, LANES), f32),
                        pltpu.SemaphoreType.DMA((2,))],
    )
    return pl.pallas_call(
        functools.partial(_combine_kernel, tm=tm),
        grid_spec=grid_spec,
        out_shape=[jax.ShapeDtypeStruct((M, D_MODEL), f32), jax.ShapeDtypeStruct((M, D_MODEL), bf16)],
        compiler_params=_cparams(("arbitrary",)),
        name="moe_combine",
    )(pos, x, wts, lng_all, lnb_all, ys)


def _routing_plan(idx, *, tm_rows, n_tiles):
    e = idx[:, :TOP_K].reshape(-1)
    onehot = (e[:, None] == jnp.arange(N_EXPERTS, dtype=jnp.int32)[None, :]).astype(jnp.int32)
    csum = jnp.cumsum(onehot, axis=0)
    counts = csum[-1]
    rank = jnp.take_along_axis(csum - onehot, e[:, None], axis=1)[:, 0]
    tiles_e = (counts + tm_rows - 1) // tm_rows
    tile_end = jnp.cumsum(tiles_e)
    row_start = (tile_end - tiles_e) * tm_rows
    pos = row_start[e] + rank
    t = jnp.arange(n_tiles, dtype=jnp.int32)
    te = jnp.sum((t[:, None] >= tile_end[None, :]).astype(jnp.int32), axis=1)
    valid = (t < tile_end[-1]).astype(jnp.int32)
    last_e = jnp.max(jnp.where(tiles_e > 0, jnp.arange(N_EXPERTS, dtype=jnp.int32), 0))
    te = jnp.where(valid == 1, jnp.minimum(te, N_EXPERTS - 1), last_e)
    return pos.astype(jnp.int32), jnp.concatenate([te, valid]).astype(jnp.int32)


def _config(B, S):
    M = B * S
    tm_moe = min(1024, M)
    return dict(
        proj_tm=min(4096, M), proj_tn=512,
        fc_ts=min(512, S),
        fox_tq=min(512, S),
        ret_c=min(256, S),
        rope_tm=min(2048, M),
        merge_tm=min(1024, M),
        ffn_tm=min(512, M), ffn_tf=1408,
        moe_tm=tm_moe, moe_tf=896,
        moe_tiles=(M * TOP_K) // tm_moe + N_EXPERTS,
        disp_tm=min(512, M), comb_tm=min(256, M),
    )


def _permute_rotary_cols(w):
    lead = w.shape[:-1]
    n = len(lead)
    w = w.reshape(lead + (RET_HEADS // 2, 2, 2, RET_QK_DIM // 2))
    w = jnp.swapaxes(w, n + 1, n + 2)
    return w.reshape(lead + (RET_QK_WIDTH,))


def _prep_in_weights(w_in, b_forget):
    w_main = jnp.concatenate([
        w_in[:, :, _O_QA:_O_F],
        _permute_rotary_cols(w_in[:, :, _O_QB:_O_KB]),
        w_in[:, :, _O_VB:],
        _permute_rotary_cols(w_in[:, :, _O_KB:_O_VB]),
    ], axis=-1).astype(bf16)
    lane_head = np.full((LANES,), -1, np.int64)
    for h in range(FOX_HEADS):
        lane_head[_c_lane(h):_c_lane(h) + C_TERMS] = h
    used = jnp.asarray(lane_head >= 0)
    src = jnp.asarray(np.maximum(lane_head, 0))
    w_f = jnp.where(used, jnp.take(w_in[:, :, _O_F:_O_QB], src, axis=2), 0.0).astype(bf16)
    b_f = jnp.where(used, jnp.take(b_forget, src, axis=1), 0.0)[:, None, :]
    return w_main, w_f, b_f


def kernel(x, positions, w_in, b_forget, ret_norm_g, w_branch_fox, w_branch_ret, w_out, ln_mix_g, ln_mix_b,
           ffn_w_gate, ffn_w_up, ffn_w_down, moe_router, moe_w_gate, moe_w_up, moe_w_down, ln_ffn_g, ln_ffn_b):
    B, S, _ = x.shape
    M = B * S
    cfg = _config(B, S)

    w_main, w_f, b_f = _prep_in_weights(w_in, b_forget)
    ng = ret_norm_g[:, None, :]
    wfox = w_branch_fox.astype(bf16)
    wret = w_branch_ret.astype(bf16)
    wout = w_out.astype(bf16)
    lmg, lmb = ln_mix_g[:, None, :], ln_mix_b[:, None, :]
    lfg, lfb = ln_ffn_g[:, None, :], ln_ffn_b[:, None, :]
    fwg, fwu, fwd = ffn_w_gate.astype(bf16), ffn_w_up.astype(bf16), ffn_w_down.astype(bf16)
    mwg, mwu, mwd = moe_w_gate.astype(bf16), moe_w_up.astype(bf16), moe_w_down.astype(bf16)
    wrt = jnp.pad(moe_router, ((0, 0), (0, 0), (0, LANES - N_EXPERTS))).astype(bf16)

    cos_t, sin_t = rope_tables(positions, tm=cfg["rope_tm"])

    xf = x.reshape(M, D_MODEL)
    xb = xf.astype(bf16)
    for layer in range(DEPTH):
        proj = in_projection(xb, w_main, layer, tm=cfg["proj_tm"], tn=cfg["proj_tn"])
        c = forget_cumsum(xb, w_f, b_f, layer, B=B, S=S, ts=cfg["fc_ts"])
        o_a = fox_attention(proj, c, B=B, S=S, tq=cfg["fox_tq"])
        o_b = retention(proj, cos_t, sin_t, ng, layer, B=B, S=S, C=cfg["ret_c"])
        i = layer // 2
        if layer % 2 == 0:
            xf, xb = merge_out(o_a, o_b, proj, xf, wfox, wret, wout, lmg, lmb, layer, tm=cfg["merge_tm"])
            xf, xb = dense_ffn(xb, xf, fwg, fwu, fwd, lfg, lfb, layer, i, tm=cfg["ffn_tm"], tf=cfg["ffn_tf"])
        else:
            xf, _, idx, wts = merge_out(o_a, o_b, proj, xf, wfox, wret, wout, lmg, lmb, layer,
                                        tm=cfg["merge_tm"], router_all=wrt, router_idx=i)
            n_tiles = cfg["moe_tiles"]
            pos, tile_info = _routing_plan(idx, tm_rows=cfg["moe_tm"], n_tiles=n_tiles)
            xs = moe_dispatch(xf, pos, n_tiles * cfg["moe_tm"], tm=cfg["disp_tm"])
            ys = moe_ffn(tile_info, xs, mwg, mwu, mwd, i, tm=cfg["moe_tm"], tf=cfg["moe_tf"])
            xf, xb = moe_combine(xf, wts, ys, pos, lfg, lfb, layer, tm=cfg["comb_tm"])
    return xf.reshape(B, S, D_MODEL)
```

```python
import functools
import math

import numpy as np
import jax
import jax.numpy as jnp
from jax import lax
from jax.experimental import pallas as pl
from jax.experimental.pallas import tpu as pltpu

f32 = jnp.float32
bf16 = jnp.bfloat16

D_MODEL = 1024
DEPTH = 4
FOX_HEADS = 8
FOX_HEAD_DIM = 64
FOX_WIDTH = FOX_HEADS * FOX_HEAD_DIM
RET_HEADS = 8
RET_QK_DIM = 64
RET_V_DIM = 128
RET_QK_WIDTH = RET_HEADS * RET_QK_DIM
RET_V_WIDTH = RET_HEADS * RET_V_DIM
ROPE_BASE = 10000.0
N_BRANCH = 2
D_FF = 2816
N_EXPERTS = 8
TOP_K = 2
D_FF_EXPERT = 3584
DN_ALPHA = (2 * DEPTH) ** 0.25
LN_EPS = 1e-5
RMS_EPS = 1e-6

LANES = 128
SUBLANES = 8
VMEM_LIMIT = 56 * 1024 * 1024

COL_QA = 0
COL_KA = 512
COL_VA = 1024
COL_QB = 1536
COL_VB = 2048
COL_GB = 3072
COL_G0 = 4096
COL_G1 = 5120
COL_KB = 6144
PROJ_COLS = 6656

_O_QA, _O_KA, _O_VA, _O_F, _O_QB, _O_KB, _O_VB, _O_GB, _O_GATE = 0, 512, 1024, 1536, 1544, 2056, 2568, 3592, 4616


def _cparams(sem, vmem=VMEM_LIMIT):
    return pltpu.CompilerParams(dimension_semantics=sem, vmem_limit_bytes=vmem)


def _mm_kernel(x_ref, w_ref, o_ref, *, tn):
    x = x_ref[...]
    for n in range(o_ref.shape[1] // tn):
        cols = slice(n * tn, (n + 1) * tn)
        o_ref[:, cols] = jnp.dot(x, w_ref[:, cols], preferred_element_type=f32).astype(o_ref.dtype)


def in_projection(xb, w_all, layer, *, tm, tn):
    M, K = xb.shape
    N = w_all.shape[2]
    return pl.pallas_call(
        functools.partial(_mm_kernel, tn=tn),
        grid=(M // tm,),
        in_specs=[pl.BlockSpec((tm, K), lambda i: (i, 0)),
                  pl.BlockSpec((None, K, N), lambda i: (layer, 0, 0), pipeline_mode=pl.Buffered(1))],
        out_specs=pl.BlockSpec((tm, N), lambda i: (i, 0)),
        out_shape=jax.ShapeDtypeStruct((M, N), bf16),
        compiler_params=_cparams(("parallel",)),
        name="in_projection",
    )(xb, w_all)


C_TERMS = 3
LOG2E = math.log2(math.e)


def _c_lane(h):
    return (FOX_HEAD_DIM if h % 2 == 0 else 0) + C_TERMS * (h // 2)


def _split_bf16(v):
    hi = v.astype(bf16)
    r1 = v - hi.astype(f32)
    mid = r1.astype(bf16)
    lo = (r1 - mid.astype(f32)).astype(bf16)
    return hi, mid, lo


def _forget_cumsum_kernel(x_ref, wf_ref, bf_ref, term_ref, o_ref, carry_ref, *, ts):
    @pl.when(pl.program_id(1) == 0)
    def _():
        carry_ref[...] = jnp.zeros_like(carry_ref)

    term = term_ref[...]
    z = jnp.dot(x_ref[...], wf_ref[...], preferred_element_type=f32) + bf_ref[...]
    lf = jnp.minimum(z, 0.0) - jnp.log1p(jnp.exp(-jnp.abs(z)))
    lf = jnp.where(term >= 0, lf, 0.0)
    row = lax.broadcasted_iota(jnp.int32, (ts, ts), 0)
    col = lax.broadcasted_iota(jnp.int32, (ts, ts), 1)
    tri = jnp.where(col <= row, 1.0, 0.0).astype(bf16)
    c = carry_ref[...]
    for part in _split_bf16(lf):
        c = c + jnp.dot(tri, part, preferred_element_type=f32)
    carry_ref[...] = c[ts - 1:ts, :]
    hi, mid, lo = _split_bf16(c * LOG2E)
    o_ref[...] = jnp.where(term == 0, hi, jnp.where(term == 1, mid, jnp.where(term == 2, lo, jnp.zeros_like(hi))))


def forget_cumsum(xb, wf_all, bf_all, layer, *, B, S, ts):
    K = xb.shape[1]
    nt = S // ts
    term = np.full((1, LANES), -1, np.int32)
    for h in range(FOX_HEADS):
        term[0, _c_lane(h):_c_lane(h) + C_TERMS] = np.arange(C_TERMS)
    return pl.pallas_call(
        functools.partial(_forget_cumsum_kernel, ts=ts),
        grid=(B, nt),
        in_specs=[pl.BlockSpec((ts, K), lambda b, j: (b * nt + j, 0)),
                  pl.BlockSpec((None, K, LANES), lambda b, j: (layer, 0, 0)),
                  pl.BlockSpec((None, 1, LANES), lambda b, j: (layer, 0, 0)),
                  pl.BlockSpec((1, LANES), lambda b, j: (0, 0))],
        out_specs=pl.BlockSpec((ts, LANES), lambda b, j: (b * nt + j, 0)),
        out_shape=jax.ShapeDtypeStruct((B * S, LANES), bf16),
        scratch_shapes=[pltpu.VMEM((1, LANES), f32)],
        compiler_params=_cparams(("parallel", "arbitrary")),
        name="forget_cumsum",
    )(xb, wf_all, bf_all, jnp.asarray(term))


def _fox_kernel(q_ref, k_ref, v_ref, ca_ref, o_ref, m_ref, acc_ref, *, tq):
    pair = pl.program_id(1)
    i = pl.program_id(2)
    lane = lax.broadcasted_iota(jnp.int32, (1, LANES), 1)
    head_lanes = (lane < FOX_HEAD_DIM, lane >= FOX_HEAD_DIM)
    q = q_ref[...] * jnp.asarray(FOX_HEAD_DIM ** -0.5 * LOG2E, bf16)
    qh = []
    for h in range(2):
        first = (FOX_HEAD_DIM if h == 0 else 0) + C_TERMS * pair
        minus_one = jnp.where((lane >= first) & (lane < first + C_TERMS), -1.0, 0.0).astype(bf16)
        qh.append(jnp.where(head_lanes[h], q, minus_one))

    m_ref[...] = jnp.full_like(m_ref, -jnp.inf)
    acc_ref[...] = jnp.zeros_like(acc_ref)

    key = lax.broadcasted_iota(jnp.int32, (tq, tq), 0)
    qry = lax.broadcasted_iota(jnp.int32, (tq, tq), 1)
    causal = key <= qry

    def chunks(js, masked):
        sts, vcs = [], []
        for j in js:
            k0 = pl.multiple_of(j * tq, tq)
            kc = k_ref[pl.ds(k0, tq), :]
            vcs.append(v_ref[pl.ds(k0, tq), :])
            ca = ca_ref[pl.ds(k0, tq), :]
            for h in range(2):
                kh = jnp.where(head_lanes[h], kc, ca)
                st = lax.dot_general(kh, qh[h], (((1,), (1,)), ((), ())), preferred_element_type=f32)
                if masked:
                    st = jnp.where(causal, st, -jnp.inf)
                sts.append(st)
        for n in range(len(js)):
            for h in range(2):
                st = sts[2 * n + h]
                m_prev = m_ref[h]
                m_new = jnp.maximum(m_prev, jnp.max(st, axis=0, keepdims=True))
                alpha = jnp.exp2(m_prev - m_new)
                p = jnp.exp2(st - m_new).astype(bf16)
                vh = jnp.where(head_lanes[h], vcs[n], jnp.ones_like(vcs[n]))
                pv = lax.dot_general(vh, p, (((0,), (0,)), ((), ())), preferred_element_type=f32)
                acc_ref[h] = alpha * acc_ref[h] + pv
                m_ref[h] = m_new

    chunks([i], True)

    @pl.when(i % 2 == 1)
    def _():
        chunks([i - 1], False)

    def body(jj, carry):
        chunks([2 * jj, 2 * jj + 1], False)
        return carry

    lax.fori_loop(0, i // 2, body, 0)

    a0, a1 = acc_ref[0], acc_ref[1]
    o0 = a0 / a0[FOX_HEAD_DIM:FOX_HEAD_DIM + 1, :]
    o1 = a1 / a1[0:1, :]
    sub = lax.broadcasted_iota(jnp.int32, (LANES, 1), 0)
    ot = jnp.where(sub < FOX_HEAD_DIM, o0, o1)
    o_ref[...] = ot.T.astype(o_ref.dtype)


def fox_attention(proj, caug, *, B, S, tq):
    nq = S // tq
    npair = FOX_HEADS // 2
    return pl.pallas_call(
        functools.partial(_fox_kernel, tq=tq),
        grid=(B, npair, nq),
        in_specs=[pl.BlockSpec((tq, LANES), lambda b, p, i: (b * nq + i, COL_QA // LANES + p)),
                  pl.BlockSpec((S, LANES), lambda b, p, i: (b, COL_KA // LANES + p)),
                  pl.BlockSpec((S, LANES), lambda b, p, i: (b, COL_VA // LANES + p)),
                  pl.BlockSpec((S, LANES), lambda b, p, i: (b, 0))],
        out_specs=pl.BlockSpec((tq, LANES), lambda b, p, i: (b * nq + i, p)),
        out_shape=jax.ShapeDtypeStruct((B * S, FOX_WIDTH), bf16),
        scratch_shapes=[pltpu.VMEM((2, 1, tq), f32), pltpu.VMEM((2, LANES, tq), f32)],
        compiler_params=_cparams(("parallel", "parallel", "arbitrary")),
        name="fox_attention",
    )(proj, proj, proj, caug)


def _rope_table_kernel(pos_ref, freq_ref, sign_ref, cos_ref, sin_ref):
    ang = pos_ref[...].astype(f32) * freq_ref[...]
    cos_ref[...] = jnp.cos(ang)
    sin_ref[...] = jnp.sin(ang) * sign_ref[...]


def rope_tables(positions, *, tm):
    n = positions.size
    half = RET_QK_DIM // 2
    inv_freq = ROPE_BASE ** (-jnp.arange(half, dtype=f32) / half)
    freq = jnp.tile(inv_freq, LANES // half).reshape(1, LANES)
    sign = jnp.asarray(np.where(np.arange(LANES) < LANES // 2, -1.0, 1.0), f32).reshape(1, LANES)
    pos = positions.reshape(n, 1)
    out = jax.ShapeDtypeStruct((n, LANES), f32)
    return pl.pallas_call(
        _rope_table_kernel,
        grid=(n // tm,),
        in_specs=[pl.BlockSpec((tm, 1), lambda i: (i, 0)),
                  pl.BlockSpec((1, LANES), lambda i: (0, 0)),
                  pl.BlockSpec((1, LANES), lambda i: (0, 0))],
        out_specs=[pl.BlockSpec((tm, LANES), lambda i: (i, 0))] * 2,
        out_shape=[out, out],
        compiler_params=_cparams(("parallel",)),
        name="rope_tables",
    )(pos, freq, sign)


def _retention_tables(C):
    h = np.arange(RET_HEADS, dtype=np.float64)
    log_gamma = np.log(1.0 - 2.0 ** (-5.0 - h))
    idx = np.arange(C, dtype=np.float64)
    diff = idx[:, None] - idx[None, :]
    scale = RET_QK_DIM ** -0.5
    decay = np.where(diff >= 0, np.exp(np.maximum(diff, 0.0)[None] * log_gamma[:, None, None]), 0.0) * scale
    lane = np.arange(LANES)
    head_lane = (lane % 64) // 32
    masks = np.stack([(head_lane == (hh % 2)) for hh in range(RET_HEADS)]).astype(np.float64)
    q_dec = np.exp((idx + 1.0)[None, :, None] * log_gamma[:, None, None]) * masks[:, None, :]
    k_dec = np.exp((C - 1.0 - idx)[None, :, None] * log_gamma[:, None, None]) * scale * masks[:, None, :]
    chunk_decay = np.exp(C * log_gamma)
    return (jnp.asarray(decay, f32), jnp.asarray(q_dec, f32), jnp.asarray(k_dec, f32),
            [float(v) for v in chunk_decay])


def _retention_kernel(q_ref, k_ref, v_ref, g_ref, cos_ref, sin_ref, dec_ref, qd_ref, kd_ref, ng_ref,
                      o_ref, state_ref, *, chunk_decay):
    @pl.when(pl.program_id(1) == 0)
    def _():
        state_ref[...] = jnp.zeros_like(state_ref)

    cos = cos_ref[...]
    sin = sin_ref[...]
    lane = lax.broadcasted_iota(jnp.int32, (1, LANES), 1)
    own = ((lane % 64) < 32, (lane % 64) >= 32)
    for p in range(RET_HEADS // 2):
        sl = slice(p * LANES, (p + 1) * LANES)
        qp = q_ref[:, sl].astype(f32)
        kp = k_ref[:, sl].astype(f32)
        qr = qp * cos + pltpu.roll(qp, LANES // 2, 1) * sin
        kr = kp * cos + pltpu.roll(kp, LANES // 2, 1) * sin
        for hh in range(2):
            h = 2 * p + hh
            vs = slice(h * RET_V_DIM, (h + 1) * RET_V_DIM)
            qm = jnp.where(own[hh], qr, 0.0).astype(bf16)
            km = jnp.where(own[hh], kr, 0.0).astype(bf16)
            sc = lax.dot_general(qm, km, (((1,), (1,)), ((), ())), preferred_element_type=f32)
            sc = (sc * dec_ref[h]).astype(bf16)
            v = v_ref[:, vs]
            intra = jnp.dot(sc, v, preferred_element_type=f32)
            st = state_ref[h]
            qd = (qr * qd_ref[h]).astype(bf16)
            cross = jnp.dot(qd, st.astype(bf16), preferred_element_type=f32)
            y = intra + cross
            kd = (kr * kd_ref[h]).astype(bf16)
            state_ref[h] = chunk_decay[h] * st + lax.dot_general(
                kd, v, (((0,), (0,)), ((), ())), preferred_element_type=f32)
            ms = jnp.mean(y * y, axis=-1, keepdims=True)
            yn = y * lax.rsqrt(ms + RMS_EPS) * ng_ref[:, vs]
            g = g_ref[:, vs].astype(f32)
            o_ref[:, vs] = (g * jax.nn.sigmoid(g) * yn).astype(o_ref.dtype)


def retention(proj, cos_t, sin_t, ng_all, layer, *, B, S, C):
    nc = S // C
    dec, q_dec, k_dec, chunk_decay = _retention_tables(C)
    row = lambda b, c: b * nc + c
    const3 = lambda b, c: (0, 0, 0)
    return pl.pallas_call(
        functools.partial(_retention_kernel, chunk_decay=chunk_decay),
        grid=(B, nc),
        in_specs=[pl.BlockSpec((C, RET_QK_WIDTH), lambda b, c: (row(b, c), COL_QB // RET_QK_WIDTH)),
                  pl.BlockSpec((C, RET_QK_WIDTH), lambda b, c: (row(b, c), COL_KB // RET_QK_WIDTH)),
                  pl.BlockSpec((C, RET_V_WIDTH), lambda b, c: (row(b, c), COL_VB // RET_V_WIDTH)),
                  pl.BlockSpec((C, RET_V_WIDTH), lambda b, c: (row(b, c), COL_GB // RET_V_WIDTH)),
                  pl.BlockSpec((C, LANES), lambda b, c: (row(b, c), 0)),
                  pl.BlockSpec((C, LANES), lambda b, c: (row(b, c), 0)),
                  pl.BlockSpec((RET_HEADS, C, C), const3),
                  pl.BlockSpec((RET_HEADS, C, LANES), const3),
                  pl.BlockSpec((RET_HEADS, C, LANES), const3),
                  pl.BlockSpec((None, 1, RET_V_WIDTH), lambda b, c: (layer, 0, 0))],
        out_specs=pl.BlockSpec((C, RET_V_WIDTH), lambda b, c: (row(b, c), 0)),
        out_shape=jax.ShapeDtypeStruct((B * S, RET_V_WIDTH), bf16),
        scratch_shapes=[pltpu.VMEM((RET_HEADS, LANES, RET_V_DIM), f32)],
        compiler_params=_cparams(("parallel", "arbitrary")),
        name="retention",
    )(proj, proj, proj, proj, cos_t, sin_t, dec, q_dec, k_dec, ng_all)


def _layer_norm(y, g, b):
    mu = jnp.mean(y, axis=-1, keepdims=True)
    d = y - mu
    var = jnp.mean(d * d, axis=-1, keepdims=True)
    return d * lax.rsqrt(var + LN_EPS) * g + b


def _top2(logits):
    lane = lax.broadcasted_iota(jnp.int32, logits.shape, 1).astype(f32)
    big = float(LANES)
    lg = jnp.where(lane < N_EXPERTS, logits, -jnp.inf)
    m1 = jnp.max(lg, axis=-1, keepdims=True)
    e1 = jnp.min(jnp.where(lg == m1, lane, big), axis=-1, keepdims=True)
    lg2 = jnp.where(lane == e1, -jnp.inf, lg)
    m2 = jnp.max(lg2, axis=-1, keepdims=True)
    e2 = jnp.min(jnp.where(lg2 == m2, lane, big), axis=-1, keepdims=True)
    ex = jnp.exp(m2 - m1)
    den = 1.0 + ex
    return e1, e2, 1.0 / den, ex / den


def _merge_kernel(*refs, route):
    if route:
        (oa_ref, ob_ref, g0_ref, g1_ref, x_ref, wf_ref, wr_ref, wo_ref, lg_ref, lb_ref, wrt_ref,
         xo_ref, xb_ref, idx_ref, wts_ref) = refs
    else:
        (oa_ref, ob_ref, g0_ref, g1_ref, x_ref, wf_ref, wr_ref, wo_ref, lg_ref, lb_ref,
         xo_ref, xb_ref) = refs
    a = jnp.dot(oa_ref[...], wf_ref[...], preferred_element_type=f32)
    b = jnp.dot(ob_ref[...], wr_ref[...], preferred_element_type=f32)
    s0 = jax.nn.sigmoid(g0_ref[...].astype(f32))
    s1 = jax.nn.sigmoid(g1_ref[...].astype(f32))
    merged = (s0 * a + s1 * b).astype(bf16)
    h = jnp.dot(merged, wo_ref[...], preferred_element_type=f32)
    xn = _layer_norm(DN_ALPHA * x_ref[...] + h, lg_ref[...], lb_ref[...])
    xo_ref[...] = xn
    xnb = xn.astype(bf16)
    xb_ref[...] = xnb
    if route:
        logits = jnp.dot(xnb, wrt_ref[...], preferred_element_type=f32)
        e1, e2, w1, w2 = _top2(logits)
        lane = lax.broadcasted_iota(jnp.int32, logits.shape, 1)
        idx_ref[...] = jnp.where(lane == 0, e1, jnp.where(lane == 1, e2, 0.0)).astype(jnp.int32)
        wts_ref[...] = jnp.where(lane == 0, w1, jnp.where(lane == 1, w2, 0.0))


def merge_out(o_a, o_b, proj, x, wfox_all, wret_all, wout_all, lng_all, lnb_all, layer, *, tm,
              router_all=None, router_idx=0):
    M = x.shape[0]
    route = router_all is not None
    lw = lambda i: (layer, 0, 0)
    in_specs = [pl.BlockSpec((tm, FOX_WIDTH), lambda i: (i, 0)),
                pl.BlockSpec((tm, RET_V_WIDTH), lambda i: (i, 0)),
                pl.BlockSpec((tm, D_MODEL), lambda i: (i, COL_G0 // D_MODEL)),
                pl.BlockSpec((tm, D_MODEL), lambda i: (i, COL_G1 // D_MODEL)),
                pl.BlockSpec((tm, D_MODEL), lambda i: (i, 0)),
                pl.BlockSpec((None, FOX_WIDTH, D_MODEL), lw),
                pl.BlockSpec((None, RET_V_WIDTH, D_MODEL), lw),
                pl.BlockSpec((None, D_MODEL, D_MODEL), lw),
                pl.BlockSpec((None, 1, D_MODEL), lw),
                pl.BlockSpec((None, 1, D_MODEL), lw)]
    args = [o_a, o_b, proj, proj, x, wfox_all, wret_all, wout_all, lng_all, lnb_all]
    out_specs = [pl.BlockSpec((tm, D_MODEL), lambda i: (i, 0)), pl.BlockSpec((tm, D_MODEL), lambda i: (i, 0))]
    out_shape = [jax.ShapeDtypeStruct((M, D_MODEL), f32), jax.ShapeDtypeStruct((M, D_MODEL), bf16)]
    if route:
        in_specs.append(pl.BlockSpec((None, D_MODEL, LANES), lambda i: (router_idx, 0, 0)))
        args.append(router_all)
        out_specs += [pl.BlockSpec((tm, LANES), lambda i: (i, 0))] * 2
        out_shape += [jax.ShapeDtypeStruct((M, LANES), jnp.int32), jax.ShapeDtypeStruct((M, LANES), f32)]
    return pl.pallas_call(
        functools.partial(_merge_kernel, route=route),
        grid=(M // tm,),
        in_specs=in_specs,
        out_specs=out_specs,
        out_shape=out_shape,
        compiler_params=_cparams(("parallel",)),
        name="merge_out",
    )(*args)


def _swiglu_partial(x, wg, wu, wd):
    g = jnp.dot(x, wg, preferred_element_type=f32)
    u = jnp.dot(x, wu, preferred_element_type=f32)
    h = (g * jax.nn.sigmoid(g) * u).astype(bf16)
    return jnp.dot(h, wd, preferred_element_type=f32)


def _dense_ffn_kernel(xb_ref, x_ref, wg_ref, wu_ref, wd_ref, lg_ref, lb_ref, xo_ref, xbo_ref, *, tf):
    x = xb_ref[...]
    h = jnp.zeros(x_ref.shape, f32)
    for c in range(wg_ref.shape[1] // tf):
        cols = slice(c * tf, (c + 1) * tf)
        h = h + _swiglu_partial(x, wg_ref[:, cols], wu_ref[:, cols], wd_ref[cols, :])
    xn = _layer_norm(DN_ALPHA * x_ref[...] + h, lg_ref[...], lb_ref[...])
    xo_ref[...] = xn
    xbo_ref[...] = xn.astype(bf16)


def dense_ffn(xb, x, wg_all, wu_all, wd_all, lng_all, lnb_all, layer, widx, *, tm, tf):
    M = x.shape[0]
    F = wg_all.shape[2]
    once = pl.Buffered(1)
    return pl.pallas_call(
        functools.partial(_dense_ffn_kernel, tf=tf),
        grid=(M // tm,),
        in_specs=[pl.BlockSpec((tm, D_MODEL), lambda i: (i, 0)),
                  pl.BlockSpec((tm, D_MODEL), lambda i: (i, 0)),
                  pl.BlockSpec((None, D_MODEL, F), lambda i: (widx, 0, 0), pipeline_mode=once),
                  pl.BlockSpec((None, D_MODEL, F), lambda i: (widx, 0, 0), pipeline_mode=once),
                  pl.BlockSpec((None, F, D_MODEL), lambda i: (widx, 0, 0), pipeline_mode=once),
                  pl.BlockSpec((None, 1, D_MODEL), lambda i: (layer, 0, 0)),
                  pl.BlockSpec((None, 1, D_MODEL), lambda i: (layer, 0, 0))],
        out_specs=[pl.BlockSpec((tm, D_MODEL), lambda i: (i, 0)),
                   pl.BlockSpec((tm, D_MODEL), lambda i: (i, 0))],
        out_shape=[jax.ShapeDtypeStruct((M, D_MODEL), f32), jax.ShapeDtypeStruct((M, D_MODEL), bf16)],
        compiler_params=_cparams(("parallel",)),
        name="dense_ffn",
    )(xb, x, wg_all, wu_all, wd_all, lng_all, lnb_all)


DMA_UNROLL = 8


def _row_slice(r):
    return pl.ds(pl.multiple_of(r * SUBLANES, SUBLANES), SUBLANES)


def _rows_to_tiles(dst_ref, val):
    tm = val.shape[0]
    for c in range(SUBLANES):
        dst_ref[pl.ds(c, tm, stride=SUBLANES), :] = val[:, c * LANES:(c + 1) * LANES]


def _tiles_to_rows(src_ref):
    tm = src_ref.shape[0] // SUBLANES
    return jnp.concatenate([src_ref[pl.ds(c, tm, stride=SUBLANES), :] for c in range(SUBLANES)], axis=-1)


def _dispatch_kernel(pos_ref, x_ref, xs_in_hbm, xs_hbm, buf_ref, sem, *, tm):
    del xs_in_hbm
    i = pl.program_id(0)
    n = pl.num_programs(0)

    def row_copy(step, slot, t, k):
        dst = pos_ref[(step * tm + t) * TOP_K + k]
        return pltpu.make_async_copy(buf_ref.at[slot, _row_slice(t), :], xs_hbm.at[_row_slice(dst), :], sem.at[slot])

    def drain(step, slot):
        def body(t, carry):
            for k in range(TOP_K):
                row_copy(step, slot, t, k).wait()
            return carry
        lax.fori_loop(0, tm, body, 0, unroll=DMA_UNROLL)

    slot = i % 2
    _rows_to_tiles(buf_ref.at[slot], x_ref[...])

    def issue(t, carry):
        for k in range(TOP_K):
            row_copy(i, slot, t, k).start(priority=k % 2)
        return carry

    lax.fori_loop(0, tm, issue, 0, unroll=DMA_UNROLL)

    @pl.when(i > 0)
    def _():
        drain(i - 1, 1 - slot)

    @pl.when(i == n - 1)
    def _():
        drain(i, slot)


def moe_dispatch(x, pos, n_rows, *, tm):
    M = x.shape[0]
    zeros = jnp.zeros((n_rows * SUBLANES, LANES), f32)
    grid_spec = pltpu.PrefetchScalarGridSpec(
        num_scalar_prefetch=1,
        grid=(M // tm,),
        in_specs=[pl.BlockSpec((tm, D_MODEL), lambda i, pos: (i, 0)),
                  pl.BlockSpec(memory_space=pl.ANY)],
        out_specs=pl.BlockSpec(memory_space=pl.ANY),
        scratch_shapes=[pltpu.VMEM((2, tm * SUBLANES, LANES), f32),
                        pltpu.SemaphoreType.DMA((2,))],
    )
    return pl.pallas_call(
        functools.partial(_dispatch_kernel, tm=tm),
        grid_spec=grid_spec,
        out_shape=jax.ShapeDtypeStruct((n_rows * SUBLANES, LANES), f32),
        input_output_aliases={2: 0},
        compiler_params=_cparams(("arbitrary",)),
        name="moe_dispatch",
    )(pos, x, zeros)


def _moe_ffn_kernel(te_ref, xs_ref, wg_ref, wu_ref, wd_ref, ys_ref, *, n_tiles, tf):
    i = pl.program_id(0)
    valid = te_ref[n_tiles + i] == 1

    @pl.when(valid)
    def _():
        x = _tiles_to_rows(xs_ref).astype(bf16)
        h = jnp.zeros(x.shape, f32)
        for c in range(wg_ref.shape[1] // tf):
            cols = slice(c * tf, (c + 1) * tf)
            h = h + _swiglu_partial(x, wg_ref[:, cols], wu_ref[:, cols], wd_ref[cols, :])
        _rows_to_tiles(ys_ref, h)

    @pl.when(jnp.logical_not(valid))
    def _():
        ys_ref[...] = jnp.zeros_like(ys_ref)


def moe_ffn(tile_info, xs, wg_all, wu_all, wd_all, widx, *, tm, tf):
    R = xs.shape[0] // SUBLANES
    n_tiles = R // tm
    F = wg_all.shape[3]
    once = pl.Buffered(1)
    grid_spec = pltpu.PrefetchScalarGridSpec(
        num_scalar_prefetch=1,
        grid=(n_tiles,),
        in_specs=[pl.BlockSpec((tm * SUBLANES, LANES), lambda i, te: (i, 0)),
                  pl.BlockSpec((None, None, D_MODEL, F), lambda i, te: (widx, te[i], 0, 0), pipeline_mode=once),
                  pl.BlockSpec((None, None, D_MODEL, F), lambda i, te: (widx, te[i], 0, 0), pipeline_mode=once),
                  pl.BlockSpec((None, None, F, D_MODEL), lambda i, te: (widx, te[i], 0, 0), pipeline_mode=once)],
        out_specs=pl.BlockSpec((tm * SUBLANES, LANES), lambda i, te: (i, 0)),
    )
    return pl.pallas_call(
        functools.partial(_moe_ffn_kernel, n_tiles=n_tiles, tf=tf),
        grid_spec=grid_spec,
        out_shape=jax.ShapeDtypeStruct((R * SUBLANES, LANES), f32),
        compiler_params=_cparams(("arbitrary",)),
        name="moe_ffn",
    )(tile_info, xs, wg_all, wu_all, wd_all)


def _combine_kernel(pos_ref, x_ref, wts_ref, lg_ref, lb_ref, ys_hbm, xo_ref, xbo_ref, buf_ref, sem, *, tm):
    i = pl.program_id(0)
    n = pl.num_programs(0)

    def row_copy(step, slot, t, k):
        src = pos_ref[(step * tm + t) * TOP_K + k]
        return pltpu.make_async_copy(ys_hbm.at[_row_slice(src), :], buf_ref.at[slot, k, _row_slice(t), :], sem.at[slot])

    def issue(step, slot):
        def body(t, carry):
            for k in range(TOP_K):
                row_copy(step, slot, t, k).start(priority=k % 2)
            return carry
        lax.fori_loop(0, tm, body, 0, unroll=DMA_UNROLL)

    slot = i % 2

    @pl.when(i == 0)
    def _():
        issue(0, 0)

    @pl.when(i + 1 < n)
    def _():
        issue(i + 1, 1 - slot)

    def drain(t, carry):
        for k in range(TOP_K):
            row_copy(i, slot, t, k).wait()
        return carry

    lax.fori_loop(0, tm, drain, 0, unroll=DMA_UNROLL)

    wts = wts_ref[...]
    h = (wts[:, 0:1] * _tiles_to_rows(buf_ref.at[slot, 0]) + wts[:, 1:2] * _tiles_to_rows(buf_ref.at[slot, 1]))
    xn = _layer_norm(DN_ALPHA * x_ref[...] + h, lg_ref[...], lb_ref[...])
    xo_ref[...] = xn
    xbo_ref[...] = xn.astype(bf16)


def moe_combine(x, wts, ys, pos, lng_all, lnb_all, layer, *, tm):
    M = x.shape[0]
    grid_spec = pltpu.PrefetchScalarGridSpec(
        num_scalar_prefetch=1,
        grid=(M // tm,),
        in_specs=[pl.BlockSpec((tm, D_MODEL), lambda i, pos: (i, 0)),
                  pl.BlockSpec((tm, LANES), lambda i, pos: (i, 0)),
                  pl.BlockSpec((None, 1, D_MODEL), lambda i, pos: (layer, 0, 0)),
                  pl.BlockSpec((None, 1, D_MODEL), lambda i, pos: (layer, 0, 0)),
                  pl.BlockSpec(memory_space=pl.ANY)],
        out_specs=[pl.BlockSpec((tm, D_MODEL), lambda i, pos: (i, 0)),
                   pl.BlockSpec((tm, D_MODEL), lambda i, pos: (i, 0))],
        scratch_shapes=[pltpu.VMEM((2, TOP_K, tm * SUBLANES, LANES), f32),
                        pltpu.SemaphoreType.DMA((2,))],
    )
    return pl.pallas_call(
        functools.partial(_combine_kernel, tm=tm),
        grid_spec=grid_spec,
        out_shape=[jax.ShapeDtypeStruct((M, D_MODEL), f32), jax.ShapeDtypeStruct((M, D_MODEL), bf16)],
        compiler_params=_cparams(("arbitrary",)),
        name="moe_combine",
    )(pos, x, wts, lng_all, lnb_all, ys)


def _routing_plan(idx, *, tm_rows, n_tiles):
    e = idx[:, :TOP_K].reshape(-1)
    onehot = (e[:, None] == jnp.arange(N_EXPERTS, dtype=jnp.int32)[None, :]).astype(jnp.int32)
    csum = jnp.cumsum(onehot, axis=0)
    counts = csum[-1]
    rank = jnp.take_along_axis(csum - onehot, e[:, None], axis=1)[:, 0]
    tiles_e = (counts + tm_rows - 1) // tm_rows
    tile_end = jnp.cumsum(tiles_e)
    row_start = (tile_end - tiles_e) * tm_rows
    pos = row_start[e] + rank
    t = jnp.arange(n_tiles, dtype=jnp.int32)
    te = jnp.sum((t[:, None] >= tile_end[None, :]).astype(jnp.int32), axis=1)
    valid = (t < tile_end[-1]).astype(jnp.int32)
    last_e = jnp.max(jnp.where(tiles_e > 0, jnp.arange(N_EXPERTS, dtype=jnp.int32), 0))
    te = jnp.where(valid == 1, jnp.minimum(te, N_EXPERTS - 1), last_e)
    return pos.astype(jnp.int32), jnp.concatenate([te, valid]).astype(jnp.int32)


def _config(B, S):
    M = B * S
    tm_moe = min(1024, M)
    return dict(
        proj_tm=min(1024, M), proj_tn=512,
        fc_ts=min(512, S),
        fox_tq=min(512, S),
        ret_c=min(256, S),
        rope_tm=min(2048, M),
        merge_tm=min(1024, M),
        ffn_tm=min(1024, M), ffn_tf=256,
        moe_tm=tm_moe, moe_tf=256,
        moe_tiles=(M * TOP_K) // tm_moe + N_EXPERTS,
        disp_tm=min(512, M), comb_tm=min(256, M),
    )


def _permute_rotary_cols(w):
    lead = w.shape[:-1]
    n = len(lead)
    w = w.reshape(lead + (RET_HEADS // 2, 2, 2, RET_QK_DIM // 2))
    w = jnp.swapaxes(w, n + 1, n + 2)
    return w.reshape(lead + (RET_QK_WIDTH,))


def _prep_in_weights(w_in, b_forget):
    w_main = jnp.concatenate([
        w_in[:, :, _O_QA:_O_F],
        _permute_rotary_cols(w_in[:, :, _O_QB:_O_KB]),
        w_in[:, :, _O_VB:],
        _permute_rotary_cols(w_in[:, :, _O_KB:_O_VB]),
    ], axis=-1).astype(bf16)
    lane_head = np.full((LANES,), -1, np.int64)
    for h in range(FOX_HEADS):
        lane_head[_c_lane(h):_c_lane(h) + C_TERMS] = h
    used = jnp.asarray(lane_head >= 0)
    src = jnp.asarray(np.maximum(lane_head, 0))
    w_f = jnp.where(used, jnp.take(w_in[:, :, _O_F:_O_QB], src, axis=2), 0.0).astype(bf16)
    b_f = jnp.where(used, jnp.take(b_forget, src, axis=1), 0.0)[:, None, :]
    return w_main, w_f, b_f


def kernel(x, positions, w_in, b_forget, ret_norm_g, w_branch_fox, w_branch_ret, w_out, ln_mix_g, ln_mix_b,
           ffn_w_gate, ffn_w_up, ffn_w_down, moe_router, moe_w_gate, moe_w_up, moe_w_down, ln_ffn_g, ln_ffn_b):
    B, S, _ = x.shape
    M = B * S
    cfg = _config(B, S)

    w_main, w_f, b_f = _prep_in_weights(w_in, b_forget)
    ng = ret_norm_g[:, None, :]
    wfox = w_branch_fox.astype(bf16)
    wret = w_branch_ret.astype(bf16)
    wout = w_out.astype(bf16)
    lmg, lmb = ln_mix_g[:, None, :], ln_mix_b[:, None, :]
    lfg, lfb = ln_ffn_g[:, None, :], ln_ffn_b[:, None, :]
    fwg, fwu, fwd = ffn_w_gate.astype(bf16), ffn_w_up.astype(bf16), ffn_w_down.astype(bf16)
    mwg, mwu, mwd = moe_w_gate.astype(bf16), moe_w_up.astype(bf16), moe_w_down.astype(bf16)
    wrt = jnp.pad(moe_router, ((0, 0), (0, 0), (0, LANES - N_EXPERTS))).astype(bf16)

    cos_t, sin_t = rope_tables(positions, tm=cfg["rope_tm"])

    xf = x.reshape(M, D_MODEL)
    xb = xf.astype(bf16)
    for layer in range(DEPTH):
        proj = in_projection(xb, w_main, layer, tm=cfg["proj_tm"], tn=cfg["proj_tn"])
        c = forget_cumsum(xb, w_f, b_f, layer, B=B, S=S, ts=cfg["fc_ts"])
        o_a = fox_attention(proj, c, B=B, S=S, tq=cfg["fox_tq"])
        o_b = retention(proj, cos_t, sin_t, ng, layer, B=B, S=S, C=cfg["ret_c"])
        i = layer // 2
        if layer % 2 == 0:
            xf, xb = merge_out(o_a, o_b, proj, xf, wfox, wret, wout, lmg, lmb, layer, tm=cfg["merge_tm"])
            xf, xb = dense_ffn(xb, xf, fwg, fwu, fwd, lfg, lfb, layer, i, tm=cfg["ffn_tm"], tf=cfg["ffn_tf"])
        else:
            xf, _, idx, wts = merge_out(o_a, o_b, proj, xf, wfox, wret, wout, lmg, lmb, layer,
                                        tm=cfg["merge_tm"], router_all=wrt, router_idx=i)
            n_tiles = cfg["moe_tiles"]
            pos, tile_info = _routing_plan(idx, tm_rows=cfg["moe_tm"], n_tiles=n_tiles)
            xs = moe_dispatch(xf, pos, n_tiles * cfg["moe_tm"], tm=cfg["disp_tm"])
            ys = moe_ffn(tile_info, xs, mwg, mwu, mwd, i, tm=cfg["moe_tm"], tf=cfg["moe_tf"])
            xf, xb = moe_combine(xf, wts, ys, pos, lfg, lfb, layer, tm=cfg["comb_tm"])
    return xf.reshape(B, S, D_MODEL)
```

```python
import functools
import math

import numpy as np
import jax
import jax.numpy as jnp
from jax import lax
from jax.experimental import pallas as pl
from jax.experimental.pallas import tpu as pltpu

f32 = jnp.float32
bf16 = jnp.bfloat16

D_MODEL = 1024
DEPTH = 4
FOX_HEADS = 8
FOX_HEAD_DIM = 64
FOX_WIDTH = FOX_HEADS * FOX_HEAD_DIM
RET_HEADS = 8
RET_QK_DIM = 64
RET_V_DIM = 128
RET_QK_WIDTH = RET_HEADS * RET_QK_DIM
RET_V_WIDTH = RET_HEADS * RET_V_DIM
ROPE_BASE = 10000.0
N_BRANCH = 2
D_FF = 2816
N_EXPERTS = 8
TOP_K = 2
D_FF_EXPERT = 3584
DN_ALPHA = (2 * DEPTH) ** 0.25
LN_EPS = 1e-5
RMS_EPS = 1e-6

LANES = 128
SUBLANES = 8
VMEM_LIMIT = 56 * 1024 * 1024

COL_QA = 0
COL_KA = 512
COL_VA = 1024
COL_QB = 1536
COL_VB = 2048
COL_GB = 3072
COL_G0 = 4096
COL_G1 = 5120
COL_KB = 6144
PROJ_COLS = 6656

_O_QA, _O_KA, _O_VA, _O_F, _O_QB, _O_KB, _O_VB, _O_GB, _O_GATE = 0, 512, 1024, 1536, 1544, 2056, 2568, 3592, 4616


def _cparams(sem, vmem=VMEM_LIMIT):
    return pltpu.CompilerParams(dimension_semantics=sem, vmem_limit_bytes=vmem)


def _mm_kernel(x_ref, w_ref, o_ref, *, tn):
    x = x_ref[...]
    for n in range(o_ref.shape[1] // tn):
        cols = slice(n * tn, (n + 1) * tn)
        o_ref[:, cols] = jnp.dot(x, w_ref[:, cols], preferred_element_type=f32).astype(o_ref.dtype)


def in_projection(xb, w_all, layer, *, tm, tn):
    M, K = xb.shape
    N = w_all.shape[2]
    return pl.pallas_call(
        functools.partial(_mm_kernel, tn=tn),
        grid=(M // tm,),
        in_specs=[pl.BlockSpec((tm, K), lambda i: (i, 0)),
                  pl.BlockSpec((None, K, N), lambda i: (layer, 0, 0), pipeline_mode=pl.Buffered(1))],
        out_specs=pl.BlockSpec((tm, N), lambda i: (i, 0)),
        out_shape=jax.ShapeDtypeStruct((M, N), bf16),
        compiler_params=_cparams(("parallel",)),
        name="in_projection",
    )(xb, w_all)


C_TERMS = 3
LOG2E = math.log2(math.e)


def _c_lane(h):
    return (FOX_HEAD_DIM if h % 2 == 0 else 0) + C_TERMS * (h // 2)


def _split_bf16(v):
    hi = v.astype(bf16)
    r1 = v - hi.astype(f32)
    mid = r1.astype(bf16)
    lo = (r1 - mid.astype(f32)).astype(bf16)
    return hi, mid, lo


def _forget_cumsum_kernel(x_ref, wf_ref, bf_ref, term_ref, o_ref, carry_ref, *, ts):
    @pl.when(pl.program_id(1) == 0)
    def _():
        carry_ref[...] = jnp.zeros_like(carry_ref)

    term = term_ref[...]
    row = lax.broadcasted_iota(jnp.int32, (ts, ts), 0)
    col = lax.broadcasted_iota(jnp.int32, (ts, ts), 1)
    tri = jnp.where(col <= row, 1.0, 0.0).astype(bf16)
    for r in range(x_ref.shape[0]):
        z = jnp.dot(x_ref[r], wf_ref[...], preferred_element_type=f32) + bf_ref[...]
        lf = jnp.minimum(z, 0.0) - jnp.log1p(jnp.exp(-jnp.abs(z)))
        lf = jnp.where(term >= 0, lf, 0.0)
        c = carry_ref[r]
        for part in _split_bf16(lf):
            c = c + jnp.dot(tri, part, preferred_element_type=f32)
        carry_ref[r] = c[ts - 1:ts, :]
        hi, mid, lo = _split_bf16(c * LOG2E)
        o_ref[r] = jnp.where(term == 0, hi, jnp.where(term == 1, mid, jnp.where(term == 2, lo, jnp.zeros_like(hi))))


def forget_cumsum(xb, wf_all, bf_all, layer, *, B, S, ts, nb):
    K = xb.shape[1]
    nt = S // ts
    term = np.full((1, LANES), -1, np.int32)
    for h in range(FOX_HEADS):
        term[0, _c_lane(h):_c_lane(h) + C_TERMS] = np.arange(C_TERMS)
    out = pl.pallas_call(
        functools.partial(_forget_cumsum_kernel, ts=ts),
        grid=(B // nb, nt),
        in_specs=[pl.BlockSpec((nb, ts, K), lambda b, j: (b, j, 0)),
                  pl.BlockSpec((None, K, LANES), lambda b, j: (layer, 0, 0)),
                  pl.BlockSpec((None, 1, LANES), lambda b, j: (layer, 0, 0)),
                  pl.BlockSpec((1, LANES), lambda b, j: (0, 0))],
        out_specs=pl.BlockSpec((nb, ts, LANES), lambda b, j: (b, j, 0)),
        out_shape=jax.ShapeDtypeStruct((B, S, LANES), bf16),
        scratch_shapes=[pltpu.VMEM((nb, 1, LANES), f32)],
        compiler_params=_cparams(("parallel", "arbitrary")),
        name="forget_cumsum",
    )(xb.reshape(B, S, K), wf_all, bf_all, jnp.asarray(term))
    return out.reshape(B * S, LANES)


def _fox_kernel(q_ref, k_ref, v_ref, ca_ref, o_ref, m_ref, acc_ref, *, tq):
    pair = pl.program_id(1)
    i = pl.program_id(2)
    lane = lax.broadcasted_iota(jnp.int32, (1, LANES), 1)
    head_lanes = (lane < FOX_HEAD_DIM, lane >= FOX_HEAD_DIM)
    q = q_ref[...] * jnp.asarray(FOX_HEAD_DIM ** -0.5 * LOG2E, bf16)
    qh = []
    for h in range(2):
        first = (FOX_HEAD_DIM if h == 0 else 0) + C_TERMS * pair
        minus_one = jnp.where((lane >= first) & (lane < first + C_TERMS), -1.0, 0.0).astype(bf16)
        qh.append(jnp.where(head_lanes[h], q, minus_one))

    m_ref[...] = jnp.full_like(m_ref, -jnp.inf)
    acc_ref[...] = jnp.zeros_like(acc_ref)

    key = lax.broadcasted_iota(jnp.int32, (tq, tq), 0)
    qry = lax.broadcasted_iota(jnp.int32, (tq, tq), 1)
    causal = key <= qry

    def chunks(js, masked):
        sts, vcs = [], []
        for j in js:
            k0 = pl.multiple_of(j * tq, tq)
            kc = k_ref[pl.ds(k0, tq), :]
            vcs.append(v_ref[pl.ds(k0, tq), :])
            ca = ca_ref[pl.ds(k0, tq), :]
            for h in range(2):
                kh = jnp.where(head_lanes[h], kc, ca)
                st = lax.dot_general(kh, qh[h], (((1,), (1,)), ((), ())), preferred_element_type=f32)
                if masked:
                    st = jnp.where(causal, st, -jnp.inf)
                sts.append(st)
        for n in range(len(js)):
            for h in range(2):
                st = sts[2 * n + h]
                m_prev = m_ref[h]
                m_new = jnp.maximum(m_prev, jnp.max(st, axis=0, keepdims=True))
                alpha = jnp.exp2(m_prev - m_new)
                p = jnp.exp2(st - m_new).astype(bf16)
                vh = jnp.where(head_lanes[h], vcs[n], jnp.ones_like(vcs[n]))
                pv = lax.dot_general(vh, p, (((0,), (0,)), ((), ())), preferred_element_type=f32)
                acc_ref[h] = alpha * acc_ref[h] + pv
                m_ref[h] = m_new

    chunks([i], True)

    def body(j, carry):
        chunks([j], False)
        return carry

    lax.fori_loop(0, i, body, 0)

    a0, a1 = acc_ref[0], acc_ref[1]
    o0 = a0 / a0[FOX_HEAD_DIM:FOX_HEAD_DIM + 1, :]
    o1 = a1 / a1[0:1, :]
    sub = lax.broadcasted_iota(jnp.int32, (LANES, 1), 0)
    ot = jnp.where(sub < FOX_HEAD_DIM, o0, o1)
    o_ref[...] = ot.T.astype(o_ref.dtype)


def fox_attention(proj, caug, *, B, S, tq):
    nq = S // tq
    npair = FOX_HEADS // 2
    return pl.pallas_call(
        functools.partial(_fox_kernel, tq=tq),
        grid=(B, npair, nq),
        in_specs=[pl.BlockSpec((tq, LANES), lambda b, p, i: (b * nq + i, COL_QA // LANES + p)),
                  pl.BlockSpec((S, LANES), lambda b, p, i: (b, COL_KA // LANES + p)),
                  pl.BlockSpec((S, LANES), lambda b, p, i: (b, COL_VA // LANES + p)),
                  pl.BlockSpec((S, LANES), lambda b, p, i: (b, 0))],
        out_specs=pl.BlockSpec((tq, LANES), lambda b, p, i: (b * nq + i, p)),
        out_shape=jax.ShapeDtypeStruct((B * S, FOX_WIDTH), bf16),
        scratch_shapes=[pltpu.VMEM((2, 1, tq), f32), pltpu.VMEM((2, LANES, tq), f32)],
        compiler_params=_cparams(("parallel", "parallel", "arbitrary")),
        name="fox_attention",
    )(proj, proj, proj, caug)


def _rope_table_kernel(pos_ref, freq_ref, sign_ref, cos_ref, sin_ref):
    ang = pos_ref[...].astype(f32) * freq_ref[...]
    cos_ref[...] = jnp.cos(ang)
    sin_ref[...] = jnp.sin(ang) * sign_ref[...]


def rope_tables(positions, *, tm):
    n = positions.size
    half = RET_QK_DIM // 2
    inv_freq = ROPE_BASE ** (-jnp.arange(half, dtype=f32) / half)
    freq = jnp.tile(inv_freq, LANES // half).reshape(1, LANES)
    sign = jnp.asarray(np.where(np.arange(LANES) < LANES // 2, -1.0, 1.0), f32).reshape(1, LANES)
    pos = positions.reshape(n, 1)
    out = jax.ShapeDtypeStruct((n, LANES), f32)
    return pl.pallas_call(
        _rope_table_kernel,
        grid=(n // tm,),
        in_specs=[pl.BlockSpec((tm, 1), lambda i: (i, 0)),
                  pl.BlockSpec((1, LANES), lambda i: (0, 0)),
                  pl.BlockSpec((1, LANES), lambda i: (0, 0))],
        out_specs=[pl.BlockSpec((tm, LANES), lambda i: (i, 0))] * 2,
        out_shape=[out, out],
        compiler_params=_cparams(("parallel",)),
        name="rope_tables",
    )(pos, freq, sign)


def _retention_tables(C):
    h = np.arange(RET_HEADS, dtype=np.float64)
    log_gamma = np.log(1.0 - 2.0 ** (-5.0 - h))
    idx = np.arange(C, dtype=np.float64)
    diff = idx[:, None] - idx[None, :]
    scale = RET_QK_DIM ** -0.5
    decay = np.where(diff >= 0, np.exp(np.maximum(diff, 0.0)[None] * log_gamma[:, None, None]), 0.0) * scale
    lane = np.arange(LANES)
    head_lane = (lane % 64) // 32
    masks = np.stack([(head_lane == (hh % 2)) for hh in range(RET_HEADS)]).astype(np.float64)
    q_dec = np.exp((idx + 1.0)[None, :, None] * log_gamma[:, None, None]) * masks[:, None, :]
    k_dec = np.exp((C - 1.0 - idx)[None, :, None] * log_gamma[:, None, None]) * scale * masks[:, None, :]
    chunk_decay = np.exp(C * log_gamma)
    return (jnp.asarray(decay, f32), jnp.asarray(q_dec, f32), jnp.asarray(k_dec, f32),
            [float(v) for v in chunk_decay])


def _retention_kernel(q_ref, k_ref, v_ref, g_ref, cos_ref, sin_ref, dec_ref, qd_ref, kd_ref, ng_ref,
                      o_ref, state_ref, *, chunk_decay):
    @pl.when(pl.program_id(1) == 0)
    def _():
        state_ref[...] = jnp.zeros_like(state_ref)

    cos = cos_ref[...]
    sin = sin_ref[...]
    lane = lax.broadcasted_iota(jnp.int32, (1, LANES), 1)
    own = ((lane % 64) < 32, (lane % 64) >= 32)
    for p in range(RET_HEADS // 2):
        sl = slice(p * LANES, (p + 1) * LANES)
        qp = q_ref[:, sl].astype(f32)
        kp = k_ref[:, sl].astype(f32)
        qr = qp * cos + pltpu.roll(qp, LANES // 2, 1) * sin
        kr = kp * cos + pltpu.roll(kp, LANES // 2, 1) * sin
        for hh in range(2):
            h = 2 * p + hh
            vs = slice(h * RET_V_DIM, (h + 1) * RET_V_DIM)
            qm = jnp.where(own[hh], qr, 0.0).astype(bf16)
            km = jnp.where(own[hh], kr, 0.0).astype(bf16)
            sc = lax.dot_general(qm, km, (((1,), (1,)), ((), ())), preferred_element_type=f32)
            sc = (sc * dec_ref[h]).astype(bf16)
            v = v_ref[:, vs]
            intra = jnp.dot(sc, v, preferred_element_type=f32)
            st = state_ref[h]
            qd = (qr * qd_ref[h]).astype(bf16)
            cross = jnp.dot(qd, st.astype(bf16), preferred_element_type=f32)
            y = intra + cross
            kd = (kr * kd_ref[h]).astype(bf16)
            state_ref[h] = chunk_decay[h] * st + lax.dot_general(
                kd, v, (((0,), (0,)), ((), ())), preferred_element_type=f32)
            ms = jnp.mean(y * y, axis=-1, keepdims=True)
            yn = y * lax.rsqrt(ms + RMS_EPS) * ng_ref[:, vs]
            g = g_ref[:, vs].astype(f32)
            o_ref[:, vs] = (g * jax.nn.sigmoid(g) * yn).astype(o_ref.dtype)


def retention(proj, cos_t, sin_t, ng_all, layer, *, B, S, C):
    nc = S // C
    dec, q_dec, k_dec, chunk_decay = _retention_tables(C)
    row = lambda b, c: b * nc + c
    const3 = lambda b, c: (0, 0, 0)
    return pl.pallas_call(
        functools.partial(_retention_kernel, chunk_decay=chunk_decay),
        grid=(B, nc),
        in_specs=[pl.BlockSpec((C, RET_QK_WIDTH), lambda b, c: (row(b, c), COL_QB // RET_QK_WIDTH)),
                  pl.BlockSpec((C, RET_QK_WIDTH), lambda b, c: (row(b, c), COL_KB // RET_QK_WIDTH)),
                  pl.BlockSpec((C, RET_V_WIDTH), lambda b, c: (row(b, c), COL_VB // RET_V_WIDTH)),
                  pl.BlockSpec((C, RET_V_WIDTH), lambda b, c: (row(b, c), COL_GB // RET_V_WIDTH)),
                  pl.BlockSpec((C, LANES), lambda b, c: (row(b, c), 0)),
                  pl.BlockSpec((C, LANES), lambda b, c: (row(b, c), 0)),
                  pl.BlockSpec((RET_HEADS, C, C), const3),
                  pl.BlockSpec((RET_HEADS, C, LANES), const3),
                  pl.BlockSpec((RET_HEADS, C, LANES), const3),
                  pl.BlockSpec((None, 1, RET_V_WIDTH), lambda b, c: (layer, 0, 0))],
        out_specs=pl.BlockSpec((C, RET_V_WIDTH), lambda b, c: (row(b, c), 0)),
        out_shape=jax.ShapeDtypeStruct((B * S, RET_V_WIDTH), bf16),
        scratch_shapes=[pltpu.VMEM((RET_HEADS, LANES, RET_V_DIM), f32)],
        compiler_params=_cparams(("parallel", "arbitrary")),
        name="retention",
    )(proj, proj, proj, proj, cos_t, sin_t, dec, q_dec, k_dec, ng_all)


def _layer_norm(y, g, b):
    mu = jnp.mean(y, axis=-1, keepdims=True)
    d = y - mu
    var = jnp.mean(d * d, axis=-1, keepdims=True)
    return d * lax.rsqrt(var + LN_EPS) * g + b


def _top2(logits):
    lane = lax.broadcasted_iota(jnp.int32, logits.shape, 1).astype(f32)
    big = float(LANES)
    lg = jnp.where(lane < N_EXPERTS, logits, -jnp.inf)
    m1 = jnp.max(lg, axis=-1, keepdims=True)
    e1 = jnp.min(jnp.where(lg == m1, lane, big), axis=-1, keepdims=True)
    lg2 = jnp.where(lane == e1, -jnp.inf, lg)
    m2 = jnp.max(lg2, axis=-1, keepdims=True)
    e2 = jnp.min(jnp.where(lg2 == m2, lane, big), axis=-1, keepdims=True)
    ex = jnp.exp(m2 - m1)
    den = 1.0 + ex
    return e1, e2, 1.0 / den, ex / den


def _merge_kernel(*refs, route):
    if route:
        (oa_ref, ob_ref, g0_ref, g1_ref, x_ref, wf_ref, wr_ref, wo_ref, lg_ref, lb_ref, wrt_ref,
         xo_ref, xb_ref, idx_ref, wts_ref) = refs
    else:
        (oa_ref, ob_ref, g0_ref, g1_ref, x_ref, wf_ref, wr_ref, wo_ref, lg_ref, lb_ref,
         xo_ref, xb_ref) = refs
    a = jnp.dot(oa_ref[...], wf_ref[...], preferred_element_type=f32)
    b = jnp.dot(ob_ref[...], wr_ref[...], preferred_element_type=f32)
    s0 = jax.nn.sigmoid(g0_ref[...].astype(f32))
    s1 = jax.nn.sigmoid(g1_ref[...].astype(f32))
    merged = (s0 * a + s1 * b).astype(bf16)
    h = jnp.dot(merged, wo_ref[...], preferred_element_type=f32)
    xn = _layer_norm(DN_ALPHA * x_ref[...] + h, lg_ref[...], lb_ref[...])
    xo_ref[...] = xn
    xnb = xn.astype(bf16)
    xb_ref[...] = xnb
    if route:
        logits = jnp.dot(xnb, wrt_ref[...], preferred_element_type=f32)
        e1, e2, w1, w2 = _top2(logits)
        lane = lax.broadcasted_iota(jnp.int32, logits.shape, 1)
        idx_ref[...] = jnp.where(lane == 0, e1, jnp.where(lane == 1, e2, 0.0)).astype(jnp.int32)
        wts_ref[...] = jnp.where(lane == 0, w1, jnp.where(lane == 1, w2, 0.0))


def merge_out(o_a, o_b, proj, x, wfox_all, wret_all, wout_all, lng_all, lnb_all, layer, *, tm,
              router_all=None, router_idx=0):
    M = x.shape[0]
    route = router_all is not None
    lw = lambda i: (layer, 0, 0)
    in_specs = [pl.BlockSpec((tm, FOX_WIDTH), lambda i: (i, 0)),
                pl.BlockSpec((tm, RET_V_WIDTH), lambda i: (i, 0)),
                pl.BlockSpec((tm, D_MODEL), lambda i: (i, COL_G0 // D_MODEL)),
                pl.BlockSpec((tm, D_MODEL), lambda i: (i, COL_G1 // D_MODEL)),
                pl.BlockSpec((tm, D_MODEL), lambda i: (i, 0)),
                pl.BlockSpec((None, FOX_WIDTH, D_MODEL), lw),
                pl.BlockSpec((None, RET_V_WIDTH, D_MODEL), lw),
                pl.BlockSpec((None, D_MODEL, D_MODEL), lw),
                pl.BlockSpec((None, 1, D_MODEL), lw),
                pl.BlockSpec((None, 1, D_MODEL), lw)]
    args = [o_a, o_b, proj, proj, x, wfox_all, wret_all, wout_all, lng_all, lnb_all]
    out_specs = [pl.BlockSpec((tm, D_MODEL), lambda i: (i, 0)), pl.BlockSpec((tm, D_MODEL), lambda i: (i, 0))]
    out_shape = [jax.ShapeDtypeStruct((M, D_MODEL), f32), jax.ShapeDtypeStruct((M, D_MODEL), bf16)]
    if route:
        in_specs.append(pl.BlockSpec((None, D_MODEL, LANES), lambda i: (router_idx, 0, 0)))
        args.append(router_all)
        out_specs += [pl.BlockSpec((tm, LANES), lambda i: (i, 0))] * 2
        out_shape += [jax.ShapeDtypeStruct((M, LANES), jnp.int32), jax.ShapeDtypeStruct((M, LANES), f32)]
    return pl.pallas_call(
        functools.partial(_merge_kernel, route=route),
        grid=(M // tm,),
        in_specs=in_specs,
        out_specs=out_specs,
        out_shape=out_shape,
        compiler_params=_cparams(("parallel",)),
        name="merge_out",
    )(*args)


def _swiglu_partial(x, wg, wu, wd):
    g = jnp.dot(x, wg, preferred_element_type=f32)
    u = jnp.dot(x, wu, preferred_element_type=f32)
    h = (g * jax.nn.sigmoid(g) * u).astype(bf16)
    return jnp.dot(h, wd, preferred_element_type=f32)


def _dense_ffn_kernel(xb_ref, x_ref, wg_ref, wu_ref, wd_ref, lg_ref, lb_ref, xo_ref, xbo_ref, *, tf):
    x = xb_ref[...]
    h = jnp.zeros(x_ref.shape, f32)
    for c in range(wg_ref.shape[1] // tf):
        cols = slice(c * tf, (c + 1) * tf)
        h = h + _swiglu_partial(x, wg_ref[:, cols], wu_ref[:, cols], wd_ref[cols, :])
    xn = _layer_norm(DN_ALPHA * x_ref[...] + h, lg_ref[...], lb_ref[...])
    xo_ref[...] = xn
    xbo_ref[...] = xn.astype(bf16)


def dense_ffn(xb, x, wg_all, wu_all, wd_all, lng_all, lnb_all, layer, widx, *, tm, tf):
    M = x.shape[0]
    F = wg_all.shape[2]
    once = pl.Buffered(1)
    return pl.pallas_call(
        functools.partial(_dense_ffn_kernel, tf=tf),
        grid=(M // tm,),
        in_specs=[pl.BlockSpec((tm, D_MODEL), lambda i: (i, 0)),
                  pl.BlockSpec((tm, D_MODEL), lambda i: (i, 0)),
                  pl.BlockSpec((None, D_MODEL, F), lambda i: (widx, 0, 0), pipeline_mode=once),
                  pl.BlockSpec((None, D_MODEL, F), lambda i: (widx, 0, 0), pipeline_mode=once),
                  pl.BlockSpec((None, F, D_MODEL), lambda i: (widx, 0, 0), pipeline_mode=once),
                  pl.BlockSpec((None, 1, D_MODEL), lambda i: (layer, 0, 0)),
                  pl.BlockSpec((None, 1, D_MODEL), lambda i: (layer, 0, 0))],
        out_specs=[pl.BlockSpec((tm, D_MODEL), lambda i: (i, 0)),
                   pl.BlockSpec((tm, D_MODEL), lambda i: (i, 0))],
        out_shape=[jax.ShapeDtypeStruct((M, D_MODEL), f32), jax.ShapeDtypeStruct((M, D_MODEL), bf16)],
        compiler_params=_cparams(("parallel",)),
        name="dense_ffn",
    )(xb, x, wg_all, wu_all, wd_all, lng_all, lnb_all)


DMA_UNROLL = 8


def _row_slice(r):
    return pl.ds(pl.multiple_of(r * SUBLANES, SUBLANES), SUBLANES)


def _rows_to_tiles(dst_ref, val):
    tm = val.shape[0]
    for c in range(SUBLANES):
        dst_ref[pl.ds(c, tm, stride=SUBLANES), :] = val[:, c * LANES:(c + 1) * LANES]


def _tiles_to_rows(src_ref):
    tm = src_ref.shape[0] // SUBLANES
    return jnp.concatenate([src_ref[pl.ds(c, tm, stride=SUBLANES), :] for c in range(SUBLANES)], axis=-1)


def _dispatch_kernel(pos_ref, x_ref, xs_in_hbm, xs_hbm, buf_ref, sem, *, tm):
    del xs_in_hbm
    i = pl.program_id(0)
    n = pl.num_programs(0)

    def row_copy(step, slot, t, k):
        dst = pos_ref[(step * tm + t) * TOP_K + k]
        return pltpu.make_async_copy(buf_ref.at[slot, _row_slice(t), :], xs_hbm.at[_row_slice(dst), :], sem.at[slot])

    def drain(step, slot):
        def body(t, carry):
            for k in range(TOP_K):
                row_copy(step, slot, t, k).wait()
            return carry
        lax.fori_loop(0, tm, body, 0, unroll=DMA_UNROLL)

    slot = i % 2
    _rows_to_tiles(buf_ref.at[slot], x_ref[...])

    def issue(t, carry):
        for k in range(TOP_K):
            row_copy(i, slot, t, k).start(priority=k % 2)
        return carry

    lax.fori_loop(0, tm, issue, 0, unroll=DMA_UNROLL)

    @pl.when(i > 0)
    def _():
        drain(i - 1, 1 - slot)

    @pl.when(i == n - 1)
    def _():
        drain(i, slot)


def moe_dispatch(x, pos, n_rows, *, tm):
    M = x.shape[0]
    zeros = jnp.zeros((n_rows * SUBLANES, LANES), f32)
    grid_spec = pltpu.PrefetchScalarGridSpec(
        num_scalar_prefetch=1,
        grid=(M // tm,),
        in_specs=[pl.BlockSpec((tm, D_MODEL), lambda i, pos: (i, 0)),
                  pl.BlockSpec(memory_space=pl.ANY)],
        out_specs=pl.BlockSpec(memory_space=pl.ANY),
        scratch_shapes=[pltpu.VMEM((2, tm * SUBLANES, LANES), f32),
                        pltpu.SemaphoreType.DMA((2,))],
    )
    return pl.pallas_call(
        functools.partial(_dispatch_kernel, tm=tm),
        grid_spec=grid_spec,
        out_shape=jax.ShapeDtypeStruct((n_rows * SUBLANES, LANES), f32),
        input_output_aliases={2: 0},
        compiler_params=_cparams(("arbitrary",)),
        name="moe_dispatch",
    )(pos, x, zeros)


def _moe_ffn_kernel(te_ref, xs_ref, wg_ref, wu_ref, wd_ref, ys_ref, *, n_tiles, tf):
    i = pl.program_id(0)
    valid = te_ref[n_tiles + i] == 1

    @pl.when(valid)
    def _():
        x = _tiles_to_rows(xs_ref).astype(bf16)
        h = jnp.zeros(x.shape, f32)
        for c in range(wg_ref.shape[1] // tf):
            cols = slice(c * tf, (c + 1) * tf)
            h = h + _swiglu_partial(x, wg_ref[:, cols], wu_ref[:, cols], wd_ref[cols, :])
        _rows_to_tiles(ys_ref, h)

    @pl.when(jnp.logical_not(valid))
    def _():
        ys_ref[...] = jnp.zeros_like(ys_ref)


def moe_ffn(tile_info, xs, wg_all, wu_all, wd_all, widx, *, tm, tf):
    R = xs.shape[0] // SUBLANES
    n_tiles = R // tm
    F = wg_all.shape[3]
    once = pl.Buffered(1)
    grid_spec = pltpu.PrefetchScalarGridSpec(
        num_scalar_prefetch=1,
        grid=(n_tiles,),
        in_specs=[pl.BlockSpec((tm * SUBLANES, LANES), lambda i, te: (i, 0)),
                  pl.BlockSpec((None, None, D_MODEL, F), lambda i, te: (widx, te[i], 0, 0), pipeline_mode=once),
                  pl.BlockSpec((None, None, D_MODEL, F), lambda i, te: (widx, te[i], 0, 0), pipeline_mode=once),
                  pl.BlockSpec((None, None, F, D_MODEL), lambda i, te: (widx, te[i], 0, 0), pipeline_mode=once)],
        out_specs=pl.BlockSpec((tm * SUBLANES, LANES), lambda i, te: (i, 0)),
    )
    return pl.pallas_call(
        functools.partial(_moe_ffn_kernel, n_tiles=n_tiles, tf=tf),
        grid_spec=grid_spec,
        out_shape=jax.ShapeDtypeStruct((R * SUBLANES, LANES), f32),
        compiler_params=_cparams(("arbitrary",)),
        name="moe_ffn",
    )(tile_info, xs, wg_all, wu_all, wd_all)


def _combine_kernel(pos_ref, x_ref, wts_ref, lg_ref, lb_ref, ys_hbm, xo_ref, xbo_ref, buf_ref, sem, *, tm):
    i = pl.program_id(0)
    n = pl.num_programs(0)

    def row_copy(step, slot, t, k):
        src = pos_ref[(step * tm + t) * TOP_K + k]
        return pltpu.make_async_copy(ys_hbm.at[_row_slice(src), :], buf_ref.at[slot, k, _row_slice(t), :], sem.at[slot])

    def issue(step, slot):
        def body(t, carry):
            for k in range(TOP_K):
                row_copy(step, slot, t, k).start(priority=k % 2)
            return carry
        lax.fori_loop(0, tm, body, 0, unroll=DMA_UNROLL)

    slot = i % 2

    @pl.when(i == 0)
    def _():
        issue(0, 0)

    @pl.when(i + 1 < n)
    def _():
        issue(i + 1, 1 - slot)

    def drain(t, carry):
        for k in range(TOP_K):
            row_copy(i, slot, t, k).wait()
        return carry

    lax.fori_loop(0, tm, drain, 0, unroll=DMA_UNROLL)

    wts = wts_ref[...]
    h = (wts[:, 0:1] * _tiles_to_rows(buf_ref.at[slot, 0]) + wts[:, 1:2] * _tiles_to_rows(buf_ref.at[slot, 1]))
    xn = _layer_norm(DN_ALPHA * x_ref[...] + h, lg_ref[...], lb_ref[...])
    xo_ref[...] = xn
    xbo_ref[...] = xn.astype(bf16)


def moe_combine(x, wts, ys, pos, lng_all, lnb_all, layer, *, tm):
    M = x.shape[0]
    grid_spec = pltpu.PrefetchScalarGridSpec(
        num_scalar_prefetch=1,
        grid=(M // tm,),
        in_specs=[pl.BlockSpec((tm, D_MODEL), lambda i, pos: (i, 0)),
                  pl.BlockSpec((tm, LANES), lambda i, pos: (i, 0)),
                  pl.BlockSpec((None, 1, D_MODEL), lambda i, pos: (layer, 0, 0)),
                  pl.BlockSpec((None, 1, D_MODEL), lambda i, pos: (layer, 0, 0)),
                  pl.BlockSpec(memory_space=pl.ANY)],
        out_specs=[pl.BlockSpec((tm, D_MODEL), lambda i, pos: (i, 0)),
                   pl.BlockSpec((tm, D_MODEL), lambda i, pos: (i, 0))],
        scratch_shapes=[pltpu.VMEM((2, TOP_K, tm * SUBLANES, LANES), f32),
                        pltpu.SemaphoreType.DMA((2,))],
    )
    return pl.pallas_call(
        functools.partial(_combine_kernel, tm=tm),
        grid_spec=grid_spec,
        out_shape=[jax.ShapeDtypeStruct((M, D_MODEL), f32), jax.ShapeDtypeStruct((M, D_MODEL), bf16)],
        compiler_params=_cparams(("arbitrary",)),
        name="moe_combine",
    )(pos, x, wts, lng_all, lnb_all, ys)


def _routing_plan(idx, *, tm_rows, n_tiles):
    e = idx[:, :TOP_K].reshape(-1)
    onehot = (e[:, None] == jnp.arange(N_EXPERTS, dtype=jnp.int32)[None, :]).astype(jnp.int32)
    csum = jnp.cumsum(onehot, axis=0)
    counts = csum[-1]
    rank = jnp.take_along_axis(csum - onehot, e[:, None], axis=1)[:, 0]
    tiles_e = (counts + tm_rows - 1) // tm_rows
    tile_end = jnp.cumsum(tiles_e)
    row_start = (tile_end - tiles_e) * tm_rows
    pos = row_start[e] + rank
    t = jnp.arange(n_tiles, dtype=jnp.int32)
    te = jnp.sum((t[:, None] >= tile_end[None, :]).astype(jnp.int32), axis=1)
    valid = (t < tile_end[-1]).astype(jnp.int32)
    last_e = jnp.max(jnp.where(tiles_e > 0, jnp.arange(N_EXPERTS, dtype=jnp.int32), 0))
    te = jnp.where(valid == 1, jnp.minimum(te, N_EXPERTS - 1), last_e)
    return pos.astype(jnp.int32), jnp.concatenate([te, valid]).astype(jnp.int32)


def _config(B, S):
    M = B * S
    tm_moe = min(1024, M)
    return dict(
        proj_tm=min(1024, M), proj_tn=512,
        fc_ts=min(512, S), fc_nb=min(4, B),
        fox_tq=min(1024, S),
        ret_c=min(256, S),
        rope_tm=min(2048, M),
        merge_tm=min(1024, M),
        ffn_tm=min(1024, M), ffn_tf=256,
        moe_tm=tm_moe, moe_tf=256,
        moe_tiles=(M * TOP_K) // tm_moe + N_EXPERTS,
        disp_tm=min(512, M), comb_tm=min(256, M),
    )


def _permute_rotary_cols(w):
    lead = w.shape[:-1]
    n = len(lead)
    w = w.reshape(lead + (RET_HEADS // 2, 2, 2, RET_QK_DIM // 2))
    w = jnp.swapaxes(w, n + 1, n + 2)
    return w.reshape(lead + (RET_QK_WIDTH,))


def _prep_in_weights(w_in, b_forget):
    w_main = jnp.concatenate([
        w_in[:, :, _O_QA:_O_F],
        _permute_rotary_cols(w_in[:, :, _O_QB:_O_KB]),
        w_in[:, :, _O_VB:],
        _permute_rotary_cols(w_in[:, :, _O_KB:_O_VB]),
    ], axis=-1).astype(bf16)
    lane_head = np.full((LANES,), -1, np.int64)
    for h in range(FOX_HEADS):
        lane_head[_c_lane(h):_c_lane(h) + C_TERMS] = h
    used = jnp.asarray(lane_head >= 0)
    src = jnp.asarray(np.maximum(lane_head, 0))
    w_f = jnp.where(used, jnp.take(w_in[:, :, _O_F:_O_QB], src, axis=2), 0.0).astype(bf16)
    b_f = jnp.where(used, jnp.take(b_forget, src, axis=1), 0.0)[:, None, :]
    return w_main, w_f, b_f


def kernel(x, positions, w_in, b_forget, ret_norm_g, w_branch_fox, w_branch_ret, w_out, ln_mix_g, ln_mix_b,
           ffn_w_gate, ffn_w_up, ffn_w_down, moe_router, moe_w_gate, moe_w_up, moe_w_down, ln_ffn_g, ln_ffn_b):
    B, S, _ = x.shape
    M = B * S
    cfg = _config(B, S)

    w_main, w_f, b_f = _prep_in_weights(w_in, b_forget)
    ng = ret_norm_g[:, None, :]
    wfox = w_branch_fox.astype(bf16)
    wret = w_branch_ret.astype(bf16)
    wout = w_out.astype(bf16)
    lmg, lmb = ln_mix_g[:, None, :], ln_mix_b[:, None, :]
    lfg, lfb = ln_ffn_g[:, None, :], ln_ffn_b[:, None, :]
    fwg, fwu, fwd = ffn_w_gate.astype(bf16), ffn_w_up.astype(bf16), ffn_w_down.astype(bf16)
    mwg, mwu, mwd = moe_w_gate.astype(bf16), moe_w_up.astype(bf16), moe_w_down.astype(bf16)
    wrt = jnp.pad(moe_router, ((0, 0), (0, 0), (0, LANES - N_EXPERTS))).astype(bf16)

    cos_t, sin_t = rope_tables(positions, tm=cfg["rope_tm"])

    xf = x.reshape(M, D_MODEL)
    xb = xf.astype(bf16)
    for layer in range(DEPTH):
        proj = in_projection(xb, w_main, layer, tm=cfg["proj_tm"], tn=cfg["proj_tn"])
        c = forget_cumsum(xb, w_f, b_f, layer, B=B, S=S, ts=cfg["fc_ts"], nb=cfg["fc_nb"])
        o_a = fox_attention(proj, c, B=B, S=S, tq=cfg["fox_tq"])
        o_b = retention(proj, cos_t, sin_t, ng, layer, B=B, S=S, C=cfg["ret_c"])
        i = layer // 2
        if layer % 2 == 0:
            xf, xb = merge_out(o_a, o_b, proj, xf, wfox, wret, wout, lmg, lmb, layer, tm=cfg["merge_tm"])
            xf, xb = dense_ffn(xb, xf, fwg, fwu, fwd, lfg, lfb, layer, i, tm=cfg["ffn_tm"], tf=cfg["ffn_tf"])
        else:
            xf, _, idx, wts = merge_out(o_a, o_b, proj, xf, wfox, wret, wout, lmg, lmb, layer,
                                        tm=cfg["merge_tm"], router_all=wrt, router_idx=i)
            n_tiles = cfg["moe_tiles"]
            pos, tile_info = _routing_plan(idx, tm_rows=cfg["moe_tm"], n_tiles=n_tiles)
            xs = moe_dispatch(xf, pos, n_tiles * cfg["moe_tm"], tm=cfg["disp_tm"])
            ys = moe_ffn(tile_info, xs, mwg, mwu, mwd, i, tm=cfg["moe_tm"], tf=cfg["moe_tf"])
            xf, xb = moe_combine(xf, wts, ys, pos, lfg, lfb, layer, tm=cfg["comb_tm"])
    return xf.reshape(B, S, D_MODEL)
```

```python
import functools
import math

import numpy as np
import jax
import jax.numpy as jnp
from jax import lax
from jax.experimental import pallas as pl
from jax.experimental.pallas import tpu as pltpu

f32 = jnp.float32
bf16 = jnp.bfloat16

D_MODEL = 1024
DEPTH = 4
FOX_HEADS = 8
FOX_HEAD_DIM = 64
FOX_WIDTH = FOX_HEADS * FOX_HEAD_DIM
RET_HEADS = 8
RET_QK_DIM = 64
RET_V_DIM = 128
RET_QK_WIDTH = RET_HEADS * RET_QK_DIM
RET_V_WIDTH = RET_HEADS * RET_V_DIM
ROPE_BASE = 10000.0
N_BRANCH = 2
D_FF = 2816
N_EXPERTS = 8
TOP_K = 2
D_FF_EXPERT = 3584
DN_ALPHA = (2 * DEPTH) ** 0.25
LN_EPS = 1e-5
RMS_EPS = 1e-6

LANES = 128
SUBLANES = 8
VMEM_LIMIT = 56 * 1024 * 1024

COL_QA = 0
COL_KA = 512
COL_VA = 1024
COL_QB = 1536
COL_VB = 2048
COL_GB = 3072
COL_G0 = 4096
COL_G1 = 5120
COL_KB = 6144
PROJ_COLS = 6656

_O_QA, _O_KA, _O_VA, _O_F, _O_QB, _O_KB, _O_VB, _O_GB, _O_GATE = 0, 512, 1024, 1536, 1544, 2056, 2568, 3592, 4616


def _cparams(sem, vmem=VMEM_LIMIT):
    return pltpu.CompilerParams(dimension_semantics=sem, vmem_limit_bytes=vmem)


def _mm_kernel(x_ref, w_ref, o_ref, *, tn):
    x = x_ref[...]
    for n in range(o_ref.shape[1] // tn):
        cols = slice(n * tn, (n + 1) * tn)
        o_ref[:, cols] = jnp.dot(x, w_ref[:, cols], preferred_element_type=f32).astype(o_ref.dtype)


def in_projection(xb, w_all, layer, *, tm, tn):
    M, K = xb.shape
    N = w_all.shape[2]
    return pl.pallas_call(
        functools.partial(_mm_kernel, tn=tn),
        grid=(M // tm,),
        in_specs=[pl.BlockSpec((tm, K), lambda i: (i, 0)),
                  pl.BlockSpec((None, K, N), lambda i: (layer, 0, 0), pipeline_mode=pl.Buffered(1))],
        out_specs=pl.BlockSpec((tm, N), lambda i: (i, 0)),
        out_shape=jax.ShapeDtypeStruct((M, N), bf16),
        compiler_params=_cparams(("parallel",)),
        name="in_projection",
    )(xb, w_all)


C_TERMS = 3
LOG2E = math.log2(math.e)


def _c_lane(h):
    return (FOX_HEAD_DIM if h % 2 == 0 else 0) + C_TERMS * (h // 2)


def _split_bf16(v):
    hi = v.astype(bf16)
    r1 = v - hi.astype(f32)
    mid = r1.astype(bf16)
    lo = (r1 - mid.astype(f32)).astype(bf16)
    return hi, mid, lo


def _forget_cumsum_kernel(x_ref, wf_ref, bf_ref, term_ref, o_ref, carry_ref, *, ts):
    @pl.when(pl.program_id(1) == 0)
    def _():
        carry_ref[...] = jnp.zeros_like(carry_ref)

    term = term_ref[...]
    row = lax.broadcasted_iota(jnp.int32, (ts, ts), 0)
    col = lax.broadcasted_iota(jnp.int32, (ts, ts), 1)
    tri = jnp.where(col <= row, 1.0, 0.0).astype(bf16)
    for r in range(x_ref.shape[0]):
        z = jnp.dot(x_ref[r], wf_ref[...], preferred_element_type=f32) + bf_ref[...]
        lf = jnp.minimum(z, 0.0) - jnp.log1p(jnp.exp(-jnp.abs(z)))
        lf = jnp.where(term >= 0, lf, 0.0)
        c = carry_ref[r]
        for part in _split_bf16(lf):
            c = c + jnp.dot(tri, part, preferred_element_type=f32)
        carry_ref[r] = c[ts - 1:ts, :]
        hi, mid, lo = _split_bf16(c * LOG2E)
        o_ref[r] = jnp.where(term == 0, hi, jnp.where(term == 1, mid, jnp.where(term == 2, lo, jnp.zeros_like(hi))))


def forget_cumsum(xb, wf_all, bf_all, layer, *, B, S, ts, nb):
    K = xb.shape[1]
    nt = S // ts
    term = np.full((1, LANES), -1, np.int32)
    for h in range(FOX_HEADS):
        term[0, _c_lane(h):_c_lane(h) + C_TERMS] = np.arange(C_TERMS)
    out = pl.pallas_call(
        functools.partial(_forget_cumsum_kernel, ts=ts),
        grid=(B // nb, nt),
        in_specs=[pl.BlockSpec((nb, ts, K), lambda b, j: (b, j, 0)),
                  pl.BlockSpec((None, K, LANES), lambda b, j: (layer, 0, 0)),
                  pl.BlockSpec((None, 1, LANES), lambda b, j: (layer, 0, 0)),
                  pl.BlockSpec((1, LANES), lambda b, j: (0, 0))],
        out_specs=pl.BlockSpec((nb, ts, LANES), lambda b, j: (b, j, 0)),
        out_shape=jax.ShapeDtypeStruct((B, S, LANES), bf16),
        scratch_shapes=[pltpu.VMEM((nb, 1, LANES), f32)],
        compiler_params=_cparams(("parallel", "arbitrary")),
        name="forget_cumsum",
    )(xb.reshape(B, S, K), wf_all, bf_all, jnp.asarray(term))
    return out.reshape(B * S, LANES)


def _fox_kernel(q_ref, k_ref, v_ref, ca_ref, o_ref, m_ref, acc_ref, *, tq):
    pair = pl.program_id(1)
    i = pl.program_id(2)
    lane = lax.broadcasted_iota(jnp.int32, (1, LANES), 1)
    head_lanes = (lane < FOX_HEAD_DIM, lane >= FOX_HEAD_DIM)
    q = q_ref[...] * jnp.asarray(FOX_HEAD_DIM ** -0.5 * LOG2E, bf16)
    qh = []
    for h in range(2):
        first = (FOX_HEAD_DIM if h == 0 else 0) + C_TERMS * pair
        minus_one = jnp.where((lane >= first) & (lane < first + C_TERMS), -1.0, 0.0).astype(bf16)
        qh.append(jnp.where(head_lanes[h], q, minus_one))

    m_ref[...] = jnp.full_like(m_ref, -jnp.inf)
    acc_ref[...] = jnp.zeros_like(acc_ref)

    key = lax.broadcasted_iota(jnp.int32, (tq, tq), 0)
    qry = lax.broadcasted_iota(jnp.int32, (tq, tq), 1)
    causal = key <= qry

    def chunks(js, masked):
        sts, vcs = [], []
        for j in js:
            k0 = pl.multiple_of(j * tq, tq)
            kc = k_ref[pl.ds(k0, tq), :]
            vcs.append(v_ref[pl.ds(k0, tq), :])
            ca = ca_ref[pl.ds(k0, tq), :]
            for h in range(2):
                kh = jnp.where(head_lanes[h], kc, ca)
                st = lax.dot_general(kh, qh[h], (((1,), (1,)), ((), ())), preferred_element_type=f32)
                if masked:
                    st = jnp.where(causal, st, -jnp.inf)
                sts.append(st)
        for n in range(len(js)):
            for h in range(2):
                st = sts[2 * n + h]
                m_prev = m_ref[h]
                m_new = jnp.maximum(m_prev, jnp.max(st, axis=0, keepdims=True))
                alpha = jnp.exp2(m_prev - m_new)
                p = jnp.exp2(st - m_new).astype(bf16)
                vh = jnp.where(head_lanes[h], vcs[n], jnp.ones_like(vcs[n]))
                pv = lax.dot_general(vh, p, (((0,), (0,)), ((), ())), preferred_element_type=f32)
                acc_ref[h] = alpha * acc_ref[h] + pv
                m_ref[h] = m_new

    chunks([i], True)

    def body(j, carry):
        chunks([j], False)
        return carry

    lax.fori_loop(0, i, body, 0)

    a0, a1 = acc_ref[0], acc_ref[1]
    o0 = a0 / a0[FOX_HEAD_DIM:FOX_HEAD_DIM + 1, :]
    o1 = a1 / a1[0:1, :]
    sub = lax.broadcasted_iota(jnp.int32, (LANES, 1), 0)
    ot = jnp.where(sub < FOX_HEAD_DIM, o0, o1)
    o_ref[...] = ot.T.astype(o_ref.dtype)


def fox_attention(proj, caug, *, B, S, tq):
    nq = S // tq
    npair = FOX_HEADS // 2
    return pl.pallas_call(
        functools.partial(_fox_kernel, tq=tq),
        grid=(B, npair, nq),
        in_specs=[pl.BlockSpec((tq, LANES), lambda b, p, i: (b * nq + i, COL_QA // LANES + p)),
                  pl.BlockSpec((S, LANES), lambda b, p, i: (b, COL_KA // LANES + p)),
                  pl.BlockSpec((S, LANES), lambda b, p, i: (b, COL_VA // LANES + p)),
                  pl.BlockSpec((S, LANES), lambda b, p, i: (b, 0))],
        out_specs=pl.BlockSpec((tq, LANES), lambda b, p, i: (b * nq + i, p)),
        out_shape=jax.ShapeDtypeStruct((B * S, FOX_WIDTH), bf16),
        scratch_shapes=[pltpu.VMEM((2, 1, tq), f32), pltpu.VMEM((2, LANES, tq), f32)],
        compiler_params=_cparams(("parallel", "parallel", "arbitrary")),
        name="fox_attention",
    )(proj, proj, proj, caug)


def _rope_table_kernel(pos_ref, freq_ref, sign_ref, cos_ref, sin_ref):
    ang = pos_ref[...].astype(f32) * freq_ref[...]
    cos_ref[...] = jnp.cos(ang)
    sin_ref[...] = jnp.sin(ang) * sign_ref[...]


def rope_tables(positions, *, tm):
    n = positions.size
    half = RET_QK_DIM // 2
    inv_freq = ROPE_BASE ** (-jnp.arange(half, dtype=f32) / half)
    freq = jnp.tile(inv_freq, LANES // half).reshape(1, LANES)
    sign = jnp.asarray(np.where(np.arange(LANES) < LANES // 2, -1.0, 1.0), f32).reshape(1, LANES)
    pos = positions.reshape(n, 1)
    out = jax.ShapeDtypeStruct((n, LANES), f32)
    return pl.pallas_call(
        _rope_table_kernel,
        grid=(n // tm,),
        in_specs=[pl.BlockSpec((tm, 1), lambda i: (i, 0)),
                  pl.BlockSpec((1, LANES), lambda i: (0, 0)),
                  pl.BlockSpec((1, LANES), lambda i: (0, 0))],
        out_specs=[pl.BlockSpec((tm, LANES), lambda i: (i, 0))] * 2,
        out_shape=[out, out],
        compiler_params=_cparams(("parallel",)),
        name="rope_tables",
    )(pos, freq, sign)


def _retention_tables(C):
    h = np.arange(RET_HEADS, dtype=np.float64)
    log_gamma = np.log(1.0 - 2.0 ** (-5.0 - h))
    idx = np.arange(C, dtype=np.float64)
    diff = idx[:, None] - idx[None, :]
    scale = RET_QK_DIM ** -0.5
    decay = np.where(diff >= 0, np.exp(np.maximum(diff, 0.0)[None] * log_gamma[:, None, None]), 0.0) * scale
    lane = np.arange(LANES)
    head_lane = (lane % 64) // 32
    masks = np.stack([(head_lane == (hh % 2)) for hh in range(RET_HEADS)]).astype(np.float64)
    q_dec = np.exp((idx + 1.0)[None, :, None] * log_gamma[:, None, None]) * masks[:, None, :]
    k_dec = np.exp((C - 1.0 - idx)[None, :, None] * log_gamma[:, None, None]) * scale * masks[:, None, :]
    chunk_decay = np.exp(C * log_gamma)
    return (jnp.asarray(decay, f32), jnp.asarray(q_dec, f32), jnp.asarray(k_dec, f32),
            [float(v) for v in chunk_decay])


def _retention_kernel(q_ref, k_ref, v_ref, g_ref, cos_ref, sin_ref, dec_ref, qd_ref, kd_ref, ng_ref,
                      o_ref, state_ref, *, chunk_decay):
    @pl.when(pl.program_id(1) == 0)
    def _():
        state_ref[...] = jnp.zeros_like(state_ref)

    cos = cos_ref[...]
    sin = sin_ref[...]
    lane = lax.broadcasted_iota(jnp.int32, (1, LANES), 1)
    own = ((lane % 64) < 32, (lane % 64) >= 32)
    for p in range(RET_HEADS // 2):
        sl = slice(p * LANES, (p + 1) * LANES)
        qp = q_ref[:, sl].astype(f32)
        kp = k_ref[:, sl].astype(f32)
        qr = qp * cos + pltpu.roll(qp, LANES // 2, 1) * sin
        kr = kp * cos + pltpu.roll(kp, LANES // 2, 1) * sin
        for hh in range(2):
            h = 2 * p + hh
            vs = slice(h * RET_V_DIM, (h + 1) * RET_V_DIM)
            qm = jnp.where(own[hh], qr, 0.0).astype(bf16)
            km = jnp.where(own[hh], kr, 0.0).astype(bf16)
            sc = lax.dot_general(qm, km, (((1,), (1,)), ((), ())), preferred_element_type=f32)
            sc = (sc * dec_ref[h]).astype(bf16)
            v = v_ref[:, vs]
            intra = jnp.dot(sc, v, preferred_element_type=f32)
            st = state_ref[h]
            qd = (qr * qd_ref[h]).astype(bf16)
            cross = jnp.dot(qd, st.astype(bf16), preferred_element_type=f32)
            y = intra + cross
            kd = (kr * kd_ref[h]).astype(bf16)
            state_ref[h] = chunk_decay[h] * st + lax.dot_general(
                kd, v, (((0,), (0,)), ((), ())), preferred_element_type=f32)
            ms = jnp.mean(y * y, axis=-1, keepdims=True)
            yn = y * lax.rsqrt(ms + RMS_EPS) * ng_ref[:, vs]
            g = g_ref[:, vs].astype(f32)
            o_ref[:, vs] = (g * jax.nn.sigmoid(g) * yn).astype(o_ref.dtype)


def retention(proj, cos_t, sin_t, ng_all, layer, *, B, S, C):
    nc = S // C
    dec, q_dec, k_dec, chunk_decay = _retention_tables(C)
    row = lambda b, c: b * nc + c
    const3 = lambda b, c: (0, 0, 0)
    return pl.pallas_call(
        functools.partial(_retention_kernel, chunk_decay=chunk_decay),
        grid=(B, nc),
        in_specs=[pl.BlockSpec((C, RET_QK_WIDTH), lambda b, c: (row(b, c), COL_QB // RET_QK_WIDTH)),
                  pl.BlockSpec((C, RET_QK_WIDTH), lambda b, c: (row(b, c), COL_KB // RET_QK_WIDTH)),
                  pl.BlockSpec((C, RET_V_WIDTH), lambda b, c: (row(b, c), COL_VB // RET_V_WIDTH)),
                  pl.BlockSpec((C, RET_V_WIDTH), lambda b, c: (row(b, c), COL_GB // RET_V_WIDTH)),
                  pl.BlockSpec((C, LANES), lambda b, c: (row(b, c), 0)),
                  pl.BlockSpec((C, LANES), lambda b, c: (row(b, c), 0)),
                  pl.BlockSpec((RET_HEADS, C, C), const3),
                  pl.BlockSpec((RET_HEADS, C, LANES), const3),
                  pl.BlockSpec((RET_HEADS, C, LANES), const3),
                  pl.BlockSpec((None, 1, RET_V_WIDTH), lambda b, c: (layer, 0, 0))],
        out_specs=pl.BlockSpec((C, RET_V_WIDTH), lambda b, c: (row(b, c), 0)),
        out_shape=jax.ShapeDtypeStruct((B * S, RET_V_WIDTH), bf16),
        scratch_shapes=[pltpu.VMEM((RET_HEADS, LANES, RET_V_DIM), f32)],
        compiler_params=_cparams(("parallel", "arbitrary")),
        name="retention",
    )(proj, proj, proj, proj, cos_t, sin_t, dec, q_dec, k_dec, ng_all)


def _layer_norm(y, g, b):
    mu = jnp.mean(y, axis=-1, keepdims=True)
    d = y - mu
    var = jnp.mean(d * d, axis=-1, keepdims=True)
    return d * lax.rsqrt(var + LN_EPS) * g + b


def _top2(logits):
    lane = lax.broadcasted_iota(jnp.int32, logits.shape, 1).astype(f32)
    big = float(LANES)
    lg = jnp.where(lane < N_EXPERTS, logits, -jnp.inf)
    m1 = jnp.max(lg, axis=-1, keepdims=True)
    e1 = jnp.min(jnp.where(lg == m1, lane, big), axis=-1, keepdims=True)
    lg2 = jnp.where(lane == e1, -jnp.inf, lg)
    m2 = jnp.max(lg2, axis=-1, keepdims=True)
    e2 = jnp.min(jnp.where(lg2 == m2, lane, big), axis=-1, keepdims=True)
    ex = jnp.exp(m2 - m1)
    den = 1.0 + ex
    return e1, e2, 1.0 / den, ex / den


def _merge_kernel(*refs, route):
    if route:
        (oa_ref, ob_ref, g0_ref, g1_ref, x_ref, wf_ref, wr_ref, wo_ref, lg_ref, lb_ref, wrt_ref,
         xo_ref, xb_ref, idx_ref, wts_ref) = refs
    else:
        (oa_ref, ob_ref, g0_ref, g1_ref, x_ref, wf_ref, wr_ref, wo_ref, lg_ref, lb_ref,
         xo_ref, xb_ref) = refs
    a = jnp.dot(oa_ref[...], wf_ref[...], preferred_element_type=f32)
    b = jnp.dot(ob_ref[...], wr_ref[...], preferred_element_type=f32)
    s0 = jax.nn.sigmoid(g0_ref[...].astype(f32))
    s1 = jax.nn.sigmoid(g1_ref[...].astype(f32))
    merged = (s0 * a + s1 * b).astype(bf16)
    h = jnp.dot(merged, wo_ref[...], preferred_element_type=f32)
    xn = _layer_norm(DN_ALPHA * x_ref[...] + h, lg_ref[...], lb_ref[...])
    xo_ref[...] = xn
    xnb = xn.astype(bf16)
    xb_ref[...] = xnb
    if route:
        logits = jnp.dot(xnb, wrt_ref[...], preferred_element_type=f32)
        e1, e2, w1, w2 = _top2(logits)
        lane = lax.broadcasted_iota(jnp.int32, logits.shape, 1)
        idx_ref[...] = jnp.where(lane == 0, e1, jnp.where(lane == 1, e2, 0.0)).astype(jnp.int32)
        wts_ref[...] = jnp.where(lane == 0, w1, jnp.where(lane == 1, w2, 0.0))


def merge_out(o_a, o_b, proj, x, wfox_all, wret_all, wout_all, lng_all, lnb_all, layer, *, tm,
              router_all=None, router_idx=0):
    M = x.shape[0]
    route = router_all is not None
    lw = lambda i: (layer, 0, 0)
    in_specs = [pl.BlockSpec((tm, FOX_WIDTH), lambda i: (i, 0)),
                pl.BlockSpec((tm, RET_V_WIDTH), lambda i: (i, 0)),
                pl.BlockSpec((tm, D_MODEL), lambda i: (i, COL_G0 // D_MODEL)),
                pl.BlockSpec((tm, D_MODEL), lambda i: (i, COL_G1 // D_MODEL)),
                pl.BlockSpec((tm, D_MODEL), lambda i: (i, 0)),
                pl.BlockSpec((None, FOX_WIDTH, D_MODEL), lw),
                pl.BlockSpec((None, RET_V_WIDTH, D_MODEL), lw),
                pl.BlockSpec((None, D_MODEL, D_MODEL), lw),
                pl.BlockSpec((None, 1, D_MODEL), lw),
                pl.BlockSpec((None, 1, D_MODEL), lw)]
    args = [o_a, o_b, proj, proj, x, wfox_all, wret_all, wout_all, lng_all, lnb_all]
    out_specs = [pl.BlockSpec((tm, D_MODEL), lambda i: (i, 0)), pl.BlockSpec((tm, D_MODEL), lambda i: (i, 0))]
    out_shape = [jax.ShapeDtypeStruct((M, D_MODEL), f32), jax.ShapeDtypeStruct((M, D_MODEL), bf16)]
    if route:
        in_specs.append(pl.BlockSpec((None, D_MODEL, LANES), lambda i: (router_idx, 0, 0)))
        args.append(router_all)
        out_specs += [pl.BlockSpec((tm, LANES), lambda i: (i, 0))] * 2
        out_shape += [jax.ShapeDtypeStruct((M, LANES), jnp.int32), jax.ShapeDtypeStruct((M, LANES), f32)]
    return pl.pallas_call(
        functools.partial(_merge_kernel, route=route),
        grid=(M // tm,),
        in_specs=in_specs,
        out_specs=out_specs,
        out_shape=out_shape,
        compiler_params=_cparams(("parallel",)),
        name="merge_out",
    )(*args)


def _swiglu_partial(x, wg, wu, wd):
    g = jnp.dot(x, wg, preferred_element_type=f32)
    u = jnp.dot(x, wu, preferred_element_type=f32)
    h = (g * jax.nn.sigmoid(g) * u).astype(bf16)
    return jnp.dot(h, wd, preferred_element_type=f32)


def _dense_ffn_kernel(xb_ref, x_ref, wg_ref, wu_ref, wd_ref, lg_ref, lb_ref, xo_ref, xbo_ref, *, tf):
    x = xb_ref[...]
    h = jnp.zeros(x_ref.shape, f32)
    for c in range(wg_ref.shape[1] // tf):
        cols = slice(c * tf, (c + 1) * tf)
        h = h + _swiglu_partial(x, wg_ref[:, cols], wu_ref[:, cols], wd_ref[cols, :])
    xn = _layer_norm(DN_ALPHA * x_ref[...] + h, lg_ref[...], lb_ref[...])
    xo_ref[...] = xn
    xbo_ref[...] = xn.astype(bf16)


def dense_ffn(xb, x, wg_all, wu_all, wd_all, lng_all, lnb_all, layer, widx, *, tm, tf):
    M = x.shape[0]
    F = wg_all.shape[2]
    once = pl.Buffered(1)
    return pl.pallas_call(
        functools.partial(_dense_ffn_kernel, tf=tf),
        grid=(M // tm,),
        in_specs=[pl.BlockSpec((tm, D_MODEL), lambda i: (i, 0)),
                  pl.BlockSpec((tm, D_MODEL), lambda i: (i, 0)),
                  pl.BlockSpec((None, D_MODEL, F), lambda i: (widx, 0, 0), pipeline_mode=once),
                  pl.BlockSpec((None, D_MODEL, F), lambda i: (widx, 0, 0), pipeline_mode=once),
                  pl.BlockSpec((None, F, D_MODEL), lambda i: (widx, 0, 0), pipeline_mode=once),
                  pl.BlockSpec((None, 1, D_MODEL), lambda i: (layer, 0, 0)),
                  pl.BlockSpec((None, 1, D_MODEL), lambda i: (layer, 0, 0))],
        out_specs=[pl.BlockSpec((tm, D_MODEL), lambda i: (i, 0)),
                   pl.BlockSpec((tm, D_MODEL), lambda i: (i, 0))],
        out_shape=[jax.ShapeDtypeStruct((M, D_MODEL), f32), jax.ShapeDtypeStruct((M, D_MODEL), bf16)],
        compiler_params=_cparams(("parallel",)),
        name="dense_ffn",
    )(xb, x, wg_all, wu_all, wd_all, lng_all, lnb_all)


DMA_UNROLL = 8


def _row_slice(r):
    return pl.ds(pl.multiple_of(r * SUBLANES, SUBLANES), SUBLANES)


def _rows_to_tiles(dst_ref, val):
    tm = val.shape[0]
    for c in range(SUBLANES):
        dst_ref[pl.ds(c, tm, stride=SUBLANES), :] = val[:, c * LANES:(c + 1) * LANES]


def _tiles_to_rows(src_ref):
    tm = src_ref.shape[0] // SUBLANES
    return jnp.concatenate([src_ref[pl.ds(c, tm, stride=SUBLANES), :] for c in range(SUBLANES)], axis=-1)


def _dispatch_kernel(pos_ref, x_ref, xs_in_hbm, xs_hbm, buf_ref, sem, *, tm):
    del xs_in_hbm
    i = pl.program_id(0)
    n = pl.num_programs(0)

    def row_copy(step, slot, t, k):
        dst = pos_ref[(step * tm + t) * TOP_K + k]
        return pltpu.make_async_copy(buf_ref.at[slot, _row_slice(t), :], xs_hbm.at[_row_slice(dst), :], sem.at[slot])

    def drain(step, slot):
        def body(t, carry):
            for k in range(TOP_K):
                row_copy(step, slot, t, k).wait()
            return carry
        lax.fori_loop(0, tm, body, 0, unroll=DMA_UNROLL)

    slot = i % 2
    _rows_to_tiles(buf_ref.at[slot], x_ref[...])

    def issue(t, carry):
        for k in range(TOP_K):
            row_copy(i, slot, t, k).start()
        return carry

    lax.fori_loop(0, tm, issue, 0, unroll=DMA_UNROLL)

    @pl.when(i > 0)
    def _():
        drain(i - 1, 1 - slot)

    @pl.when(i == n - 1)
    def _():
        drain(i, slot)


def moe_dispatch(x, pos, n_rows, *, tm):
    M = x.shape[0]
    zeros = jnp.zeros((n_rows * SUBLANES, LANES), f32)
    grid_spec = pltpu.PrefetchScalarGridSpec(
        num_scalar_prefetch=1,
        grid=(M // tm,),
        in_specs=[pl.BlockSpec((tm, D_MODEL), lambda i, pos: (i, 0)),
                  pl.BlockSpec(memory_space=pl.ANY)],
        out_specs=pl.BlockSpec(memory_space=pl.ANY),
        scratch_shapes=[pltpu.VMEM((2, tm * SUBLANES, LANES), f32),
                        pltpu.SemaphoreType.DMA((2,))],
    )
    return pl.pallas_call(
        functools.partial(_dispatch_kernel, tm=tm),
        grid_spec=grid_spec,
        out_shape=jax.ShapeDtypeStruct((n_rows * SUBLANES, LANES), f32),
        input_output_aliases={2: 0},
        compiler_params=_cparams(("arbitrary",)),
        name="moe_dispatch",
    )(pos, x, zeros)


def _moe_ffn_kernel(te_ref, xs_ref, wg_ref, wu_ref, wd_ref, ys_ref, *, n_tiles, tf):
    i = pl.program_id(0)
    valid = te_ref[n_tiles + i] == 1

    @pl.when(valid)
    def _():
        x = _tiles_to_rows(xs_ref).astype(bf16)
        h = jnp.zeros(x.shape, f32)
        for c in range(wg_ref.shape[1] // tf):
            cols = slice(c * tf, (c + 1) * tf)
            h = h + _swiglu_partial(x, wg_ref[:, cols], wu_ref[:, cols], wd_ref[cols, :])
        _rows_to_tiles(ys_ref, h)

    @pl.when(jnp.logical_not(valid))
    def _():
        ys_ref[...] = jnp.zeros_like(ys_ref)


def moe_ffn(tile_info, xs, wg_all, wu_all, wd_all, widx, *, tm, tf):
    R = xs.shape[0] // SUBLANES
    n_tiles = R // tm
    F = wg_all.shape[3]
    once = pl.Buffered(1)
    grid_spec = pltpu.PrefetchScalarGridSpec(
        num_scalar_prefetch=1,
        grid=(n_tiles,),
        in_specs=[pl.BlockSpec((tm * SUBLANES, LANES), lambda i, te: (i, 0)),
                  pl.BlockSpec((None, None, D_MODEL, F), lambda i, te: (widx, te[i], 0, 0), pipeline_mode=once),
                  pl.BlockSpec((None, None, D_MODEL, F), lambda i, te: (widx, te[i], 0, 0), pipeline_mode=once),
                  pl.BlockSpec((None, None, F, D_MODEL), lambda i, te: (widx, te[i], 0, 0), pipeline_mode=once)],
        out_specs=pl.BlockSpec((tm * SUBLANES, LANES), lambda i, te: (i, 0)),
    )
    return pl.pallas_call(
        functools.partial(_moe_ffn_kernel, n_tiles=n_tiles, tf=tf),
        grid_spec=grid_spec,
        out_shape=jax.ShapeDtypeStruct((R * SUBLANES, LANES), f32),
        compiler_params=_cparams(("arbitrary",)),
        name="moe_ffn",
    )(tile_info, xs, wg_all, wu_all, wd_all)


def _combine_kernel(pos_ref, x_ref, wts_ref, lg_ref, lb_ref, ys_hbm, xo_ref, xbo_ref, buf_ref, sem, *, tm):
    i = pl.program_id(0)
    n = pl.num_programs(0)

    def row_copy(step, slot, t, k):
        src = pos_ref[(step * tm + t) * TOP_K + k]
        return pltpu.make_async_copy(ys_hbm.at[_row_slice(src), :], buf_ref.at[slot, k, _row_slice(t), :], sem.at[slot])

    def issue(step, slot):
        def body(t, carry):
            for k in range(TOP_K):
                row_copy(step, slot, t, k).start()
            return carry
        lax.fori_loop(0, tm, body, 0, unroll=DMA_UNROLL)

    slot = i % 2

    @pl.when(i == 0)
    def _():
        issue(0, 0)

    @pl.when(i + 1 < n)
    def _():
        issue(i + 1, 1 - slot)

    def drain(t, carry):
        for k in range(TOP_K):
            row_copy(i, slot, t, k).wait()
        return carry

    lax.fori_loop(0, tm, drain, 0, unroll=DMA_UNROLL)

    wts = wts_ref[...]
    h = (wts[:, 0:1] * _tiles_to_rows(buf_ref.at[slot, 0]) + wts[:, 1:2] * _tiles_to_rows(buf_ref.at[slot, 1]))
    xn = _layer_norm(DN_ALPHA * x_ref[...] + h, lg_ref[...], lb_ref[...])
    xo_ref[...] = xn
    xbo_ref[...] = xn.astype(bf16)


def moe_combine(x, wts, ys, pos, lng_all, lnb_all, layer, *, tm):
    M = x.shape[0]
    grid_spec = pltpu.PrefetchScalarGridSpec(
        num_scalar_prefetch=1,
        grid=(M // tm,),
        in_specs=[pl.BlockSpec((tm, D_MODEL), lambda i, pos: (i, 0)),
                  pl.BlockSpec((tm, LANES), lambda i, pos: (i, 0)),
                  pl.BlockSpec((None, 1, D_MODEL), lambda i, pos: (layer, 0, 0)),
                  pl.BlockSpec((None, 1, D_MODEL), lambda i, pos: (layer, 0, 0)),
                  pl.BlockSpec(memory_space=pl.ANY)],
        out_specs=[pl.BlockSpec((tm, D_MODEL), lambda i, pos: (i, 0)),
                   pl.BlockSpec((tm, D_MODEL), lambda i, pos: (i, 0))],
        scratch_shapes=[pltpu.VMEM((2, TOP_K, tm * SUBLANES, LANES), f32),
                        pltpu.SemaphoreType.DMA((2,))],
    )
    return pl.pallas_call(
        functools.partial(_combine_kernel, tm=tm),
        grid_spec=grid_spec,
        out_shape=[jax.ShapeDtypeStruct((M, D_MODEL), f32), jax.ShapeDtypeStruct((M, D_MODEL), bf16)],
        compiler_params=_cparams(("arbitrary",)),
        name="moe_combine",
    )(pos, x, wts, lng_all, lnb_all, ys)


def _routing_plan(idx, *, tm_rows, n_tiles):
    e = idx[:, :TOP_K].reshape(-1)
    onehot = (e[:, None] == jnp.arange(N_EXPERTS, dtype=jnp.int32)[None, :]).astype(jnp.int32)
    csum = jnp.cumsum(onehot, axis=0)
    counts = csum[-1]
    rank = jnp.take_along_axis(csum - onehot, e[:, None], axis=1)[:, 0]
    tiles_e = (counts + tm_rows - 1) // tm_rows
    tile_end = jnp.cumsum(tiles_e)
    row_start = (tile_end - tiles_e) * tm_rows
    pos = row_start[e] + rank
    t = jnp.arange(n_tiles, dtype=jnp.int32)
    te = jnp.sum((t[:, None] >= tile_end[None, :]).astype(jnp.int32), axis=1)
    valid = (t < tile_end[-1]).astype(jnp.int32)
    last_e = jnp.max(jnp.where(tiles_e > 0, jnp.arange(N_EXPERTS, dtype=jnp.int32), 0))
    te = jnp.where(valid == 1, jnp.minimum(te, N_EXPERTS - 1), last_e)
    return pos.astype(jnp.int32), jnp.concatenate([te, valid]).astype(jnp.int32)


def _config(B, S):
    M = B * S
    tm_moe = min(1024, M)
    return dict(
        proj_tm=min(1024, M), proj_tn=512,
        fc_ts=min(512, S), fc_nb=min(8, B),
        fox_tq=min(1024, S),
        ret_c=min(256, S),
        rope_tm=min(2048, M),
        merge_tm=min(1024, M),
        ffn_tm=min(1024, M), ffn_tf=256,
        moe_tm=tm_moe, moe_tf=256,
        moe_tiles=(M * TOP_K) // tm_moe + N_EXPERTS,
        disp_tm=min(512, M), comb_tm=min(256, M),
    )


def _permute_rotary_cols(w):
    lead = w.shape[:-1]
    n = len(lead)
    w = w.reshape(lead + (RET_HEADS // 2, 2, 2, RET_QK_DIM // 2))
    w = jnp.swapaxes(w, n + 1, n + 2)
    return w.reshape(lead + (RET_QK_WIDTH,))


def _prep_in_weights(w_in, b_forget):
    w_b = w_in.astype(bf16)
    w_main = jnp.concatenate([
        w_b[:, :, _O_QA:_O_F],
        _permute_rotary_cols(w_b[:, :, _O_QB:_O_KB]),
        w_b[:, :, _O_VB:],
        _permute_rotary_cols(w_b[:, :, _O_KB:_O_VB]),
    ], axis=-1)
    lane_head = np.full((LANES,), -1, np.int64)
    for h in range(FOX_HEADS):
        lane_head[_c_lane(h):_c_lane(h) + C_TERMS] = h
    used = jnp.asarray(lane_head >= 0)
    src = jnp.asarray(np.maximum(lane_head, 0))
    w_f = jnp.where(used, jnp.take(w_in[:, :, _O_F:_O_QB], src, axis=2), 0.0).astype(bf16)
    b_f = jnp.where(used, jnp.take(b_forget, src, axis=1), 0.0)[:, None, :]
    return w_main, w_f, b_f


def kernel(x, positions, w_in, b_forget, ret_norm_g, w_branch_fox, w_branch_ret, w_out, ln_mix_g, ln_mix_b,
           ffn_w_gate, ffn_w_up, ffn_w_down, moe_router, moe_w_gate, moe_w_up, moe_w_down, ln_ffn_g, ln_ffn_b):
    B, S, _ = x.shape
    M = B * S
    cfg = _config(B, S)

    w_main, w_f, b_f = _prep_in_weights(w_in, b_forget)
    ng = ret_norm_g[:, None, :]
    wfox = w_branch_fox.astype(bf16)
    wret = w_branch_ret.astype(bf16)
    wout = w_out.astype(bf16)
    lmg, lmb = ln_mix_g[:, None, :], ln_mix_b[:, None, :]
    lfg, lfb = ln_ffn_g[:, None, :], ln_ffn_b[:, None, :]
    fwg, fwu, fwd = ffn_w_gate.astype(bf16), ffn_w_up.astype(bf16), ffn_w_down.astype(bf16)
    mwg, mwu, mwd = moe_w_gate.astype(bf16), moe_w_up.astype(bf16), moe_w_down.astype(bf16)
    wrt = jnp.pad(moe_router, ((0, 0), (0, 0), (0, LANES - N_EXPERTS))).astype(bf16)

    cos_t, sin_t = rope_tables(positions, tm=cfg["rope_tm"])

    xf = x.reshape(M, D_MODEL)
    xb = xf.astype(bf16)
    for layer in range(DEPTH):
        proj = in_projection(xb, w_main, layer, tm=cfg["proj_tm"], tn=cfg["proj_tn"])
        c = forget_cumsum(xb, w_f, b_f, layer, B=B, S=S, ts=cfg["fc_ts"], nb=cfg["fc_nb"])
        o_a = fox_attention(proj, c, B=B, S=S, tq=cfg["fox_tq"])
        o_b = retention(proj, cos_t, sin_t, ng, layer, B=B, S=S, C=cfg["ret_c"])
        i = layer // 2
        if layer % 2 == 0:
            xf, xb = merge_out(o_a, o_b, proj, xf, wfox, wret, wout, lmg, lmb, layer, tm=cfg["merge_tm"])
            xf, xb = dense_ffn(xb, xf, fwg, fwu, fwd, lfg, lfb, layer, i, tm=cfg["ffn_tm"], tf=cfg["ffn_tf"])
        else:
            xf, _, idx, wts = merge_out(o_a, o_b, proj, xf, wfox, wret, wout, lmg, lmb, layer,
                                        tm=cfg["merge_tm"], router_all=wrt, router_idx=i)
            n_tiles = cfg["moe_tiles"]
            pos, tile_info = _routing_plan(idx, tm_rows=cfg["moe_tm"], n_tiles=n_tiles)
            xs = moe_dispatch(xf, pos, n_tiles * cfg["moe_tm"], tm=cfg["disp_tm"])
            ys = moe_ffn(tile_info, xs, mwg, mwu, mwd, i, tm=cfg["moe_tm"], tf=cfg["moe_tf"])
            xf, xb = moe_combine(xf, wts, ys, pos, lfg, lfb, layer, tm=cfg["comb_tm"])
    return xf.reshape(B, S, D_MODEL)
```
